```python
import math
import jax, jax.numpy as jnp
from jax import lax
import numpy as np

D_MODEL = 4096
BATCH = 2
SEQ = 8192
DEPTH = 4
DEC_BATCH = 1
DEC_SEQ = 16384
PAST_LEN = 128

N_MIXERS = 2
N_MLSTM_LAYERS = (DEPTH + N_MIXERS - 1) // N_MIXERS
N_MLA_LAYERS = DEPTH // N_MIXERS

ML_HEADS = 8
ML_DQK = D_MODEL // ML_HEADS // 2
ML_DV = D_MODEL // ML_HEADS
ML_CHUNK = 64
ML_QK_W = ML_HEADS * ML_DQK
ML_V_W = ML_HEADS * ML_DV
ML_N_GATES = 4
ML_IN_W = 2 * ML_QK_W + 2 * ML_V_W + ML_N_GATES * ML_HEADS
ML_M_INIT = -1e30

MLA_HEADS = 32
MLA_NOPE = 128
MLA_ROPE = 64
MLA_V = 128
MLA_Q_RANK = 1024
MLA_KV_RANK = 512
MLA_IN_W = MLA_Q_RANK + MLA_KV_RANK + MLA_ROPE
MLA_Q_BLOCK = 128
ROPE_BASE = 10000.0

D_FF = 11008
CONV_W = 3

DN_ALPHA = (2 * DEPTH) ** 0.25
DN_BETA = (8 * DEPTH) ** -0.25
LN_EPS = 1e-5
RMS_EPS = 1e-6

kernel_name = 'hybrid_mlstm_mla_deepnorm_encoder'

F32 = jnp.float32


def layer_norm(x, g, b):
    xf = x.astype(F32)
    mu = jnp.mean(xf, -1, keepdims=True)
    var = jnp.mean(jnp.square(xf - mu), -1, keepdims=True)
    return ((xf - mu) * lax.rsqrt(var + LN_EPS) * g + b).astype(x.dtype)


def rms_norm(x, g):
    xf = x.astype(F32)
    return (xf * lax.rsqrt(jnp.mean(xf * xf, -1, keepdims=True) + RMS_EPS) * g).astype(x.dtype)


def mlstm_scan(q, k, v, ig, lf):
    B, H, S, _ = q.shape
    nc = S // ML_CHUNK

    def chunks(t):
        t = t.reshape(B, H, nc, ML_CHUNK, *t.shape[3:])
        return jnp.moveaxis(t, 2, 0)

    qc, kc, vc, igc, lfc = (chunks(t) for t in (q, k, v, ig, lf))
    bc = jnp.cumsum(lfc, axis=-1)
    lower = jnp.tril(jnp.ones((ML_CHUNK, ML_CHUNK), bool))

    def step(carry, xs):
        C, n, m = carry
        q_, k_, v_, i_, b_ = xs
        logd = jnp.where(lower, b_[..., :, None] - b_[..., None, :] + i_[..., None, :], -jnp.inf)
        inter = b_ + m[..., None]
        m_row = jnp.maximum(jnp.max(logd, -1), inter)
        s = jnp.einsum('bhld,bhsd->bhls', q_, k_) * jnp.exp(logd - m_row[..., None])
        w_inter = jnp.exp(inter - m_row)
        num = jnp.einsum('bhls,bhsv->bhlv', s, v_) + w_inter[..., None] * jnp.einsum('bhvd,bhld->bhlv', C, q_)
        den = jnp.sum(s, -1) + w_inter * jnp.einsum('bhd,bhld->bhl', n, q_)
        h = num / jnp.maximum(jnp.abs(den), jnp.exp(-m_row))[..., None]
        bl = b_[..., -1]
        a = bl[..., None] - b_ + i_
        m_new = jnp.maximum(bl + m, jnp.max(a, -1))
        decay = jnp.exp(bl + m - m_new)
        wk = jnp.exp(a - m_new[..., None])[..., None] * k_
        C = decay[..., None, None] * C + jnp.einsum('bhsv,bhsd->bhvd', v_, wk)
        n = decay[..., None] * n + jnp.sum(wk, -2)
        return (C, n, m_new), h

    init = (jnp.zeros((B, H, ML_DV, ML_DQK), F32),
            jnp.zeros((B, H, ML_DQK), F32),
            jnp.full((B, H), ML_M_INIT, F32))
    _, hs = lax.scan(step, init, (qc, kc, vc, igc, bc))
    return jnp.moveaxis(hs, 0, 2).reshape(B, H, S, ML_DV)


def mlstm_mixer(x, w_in, b_gates, norm_g, w_out):
    B, S, _ = x.shape
    proj = x @ w_in
    q, k, v, o, g = jnp.split(proj, [ML_QK_W, 2 * ML_QK_W, 2 * ML_QK_W + ML_V_W, 2 * ML_QK_W + 2 * ML_V_W], axis=-1)

    def heads(t, d):
        return t.reshape(B, S, ML_HEADS, d).transpose(0, 2, 1, 3).astype(F32)

    q = heads(q, ML_DQK)
    k = heads(k, ML_DQK) * (ML_DQK ** -0.5)
    v = heads(v, ML_DV)
    g = (g.astype(F32) + b_gates).reshape(B, S, ML_N_GATES, ML_HEADS).transpose(2, 0, 3, 1)
    i_fw, f_fw, i_bw, f_bw = g[0], g[1], g[2], g[3]
    h_fw = mlstm_scan(q, k, v, i_fw, jax.nn.log_sigmoid(f_fw))
    h_bw = mlstm_scan(q[:, :, ::-1], k[:, :, ::-1], v[:, :, ::-1],
                      i_bw[..., ::-1], jax.nn.log_sigmoid(f_bw)[..., ::-1])[:, :, ::-1]
    h = h_fw + h_bw
    h = h * lax.rsqrt(jnp.mean(h * h, -1, keepdims=True) + RMS_EPS)
    h = h.transpose(0, 2, 1, 3).reshape(B, S, ML_V_W) * norm_g
    h = (h * jax.nn.sigmoid(o.astype(F32))).astype(x.dtype)
    return h @ w_out


def rotary(t, cos, sin):
    t1, t2 = jnp.split(t.astype(F32), 2, axis=-1)
    return jnp.concatenate([t1 * cos - t2 * sin, t1 * sin + t2 * cos], -1).astype(t.dtype)


def mla_mixer(x, w_in, q_norm_g, kv_norm_g, w_uq, w_ukv, w_out):
    B, S, _ = x.shape
    c = x @ w_in
    cq, ckv, kr = jnp.split(c, [MLA_Q_RANK, MLA_Q_RANK + MLA_KV_RANK], axis=-1)
    cq = rms_norm(cq, q_norm_g)
    ckv = rms_norm(ckv, kv_norm_g)
    q = (cq @ w_uq).reshape(B, S, MLA_HEADS, MLA_NOPE + MLA_ROPE)
    kv = (ckv @ w_ukv).reshape(B, S, MLA_HEADS, MLA_NOPE + MLA_V)
    q_nope, q_rope = jnp.split(q, [MLA_NOPE], axis=-1)
    k_nope, v = jnp.split(kv, [MLA_NOPE], axis=-1)
    inv_freq = jnp.power(ROPE_BASE, -jnp.arange(0, MLA_ROPE, 2, dtype=F32) / MLA_ROPE)
    ang = jnp.arange(S, dtype=F32)[:, None] * inv_freq[None, :]
    cos, sin = jnp.cos(ang), jnp.sin(ang)
    q_rope = rotary(q_rope, cos[:, None, :], sin[:, None, :])
    kr = rotary(kr, cos, sin)
    scale = (MLA_NOPE + MLA_ROPE) ** -0.5
    nb = S // MLA_Q_BLOCK
    qn_b = q_nope.reshape(B, nb, MLA_Q_BLOCK, MLA_HEADS, MLA_NOPE).swapaxes(0, 1)
    qr_b = q_rope.reshape(B, nb, MLA_Q_BLOCK, MLA_HEADS, MLA_ROPE).swapaxes(0, 1)

    def attend(blk):
        qn, qr = blk
        s = (jnp.einsum('bqhd,bkhd->bhqk', qn, k_nope) + jnp.einsum('bqhr,bkr->bhqk', qr, kr)).astype(F32) * scale
        p = jax.nn.softmax(s, axis=-1).astype(v.dtype)
        return jnp.einsum('bhqk,bkhd->bqhd', p, v)

    o = lax.map(attend, (qn_b, qr_b))
    o = o.swapaxes(0, 1).reshape(B, S, MLA_HEADS * MLA_V)
    return o @ w_out


def conv_ffn(x, w_up, conv_w, conv_b, w_down):
    S = x.shape[1]
    h = x @ w_up
    hp = jnp.pad(h, ((0, 0), (CONV_W // 2, CONV_W // 2), (0, 0)))
    h = sum(hp[:, j:j + S] * conv_w[j] for j in range(CONV_W)) + conv_b
    g, u = jnp.split(h, 2, axis=-1)
    return (jax.nn.silu(g) * u) @ w_down


def trunk(x, ml_w_in, ml_b_gates, ml_norm_g, ml_w_out, mla_w_in, mla_q_norm_g, mla_kv_norm_g,
          mla_w_uq, mla_w_ukv, mla_w_out, ffn_w_up, ffn_conv_w, ffn_conv_b, ffn_w_down,
          ln1_g, ln1_b, ln2_g, ln2_b):
    for i in range(DEPTH):
        j = i // N_MIXERS
        if i % N_MIXERS == 0:
            mix = mlstm_mixer(x, ml_w_in[j], ml_b_gates[j], ml_norm_g[j], ml_w_out[j])
        else:
            mix = mla_mixer(x, mla_w_in[j], mla_q_norm_g[j], mla_kv_norm_g[j],
                            mla_w_uq[j], mla_w_ukv[j], mla_w_out[j])
        x = layer_norm(DN_ALPHA * x + mix, ln1_g[i], ln1_b[i])
        x = layer_norm(DN_ALPHA * x + conv_ffn(x, ffn_w_up[i], ffn_conv_w[i], ffn_conv_b[i], ffn_w_down[i]),
                       ln2_g[i], ln2_b[i])
    return x


def setup_inputs(seed: int = 0) -> dict:
    key = jax.random.key(seed)
    ks = iter(jax.random.split(key, 32))

    def nrm(shape, scale):
        return jax.random.normal(next(ks), shape, F32) * scale

    H = ML_HEADS
    x_prompt = nrm((BATCH, SEQ, D_MODEL), 1.0)
    x_sample = nrm((DEC_BATCH, DEC_SEQ, D_MODEL), 1.0)
    ml_w_in = nrm((N_MLSTM_LAYERS, D_MODEL, ML_IN_W), D_MODEL ** -0.5)
    gate_center = jnp.concatenate([jnp.zeros((H,), F32), jnp.full((H,), 3.0, F32),
                                   jnp.zeros((H,), F32), jnp.full((H,), 3.0, F32)])
    ml_b_gates = gate_center + nrm((N_MLSTM_LAYERS, ML_N_GATES * H), 0.5)
    ml_norm_g = 1.0 + nrm((N_MLSTM_LAYERS, ML_V_W), 0.02)
    ml_w_out = nrm((N_MLSTM_LAYERS, ML_V_W, D_MODEL), ML_V_W ** -0.5 * DN_BETA)
    mla_w_in = nrm((N_MLA_LAYERS, D_MODEL, MLA_IN_W), D_MODEL ** -0.5)
    mla_q_norm_g = 1.0 + nrm((N_MLA_LAYERS, MLA_Q_RANK), 0.02)
    mla_kv_norm_g = 1.0 + nrm((N_MLA_LAYERS, MLA_KV_RANK), 0.02)
    mla_w_uq = nrm((N_MLA_LAYERS, MLA_Q_RANK, MLA_HEADS * (MLA_NOPE + MLA_ROPE)), MLA_Q_RANK ** -0.5)
    mla_w_ukv = nrm((N_MLA_LAYERS, MLA_KV_RANK, MLA_HEADS * (MLA_NOPE + MLA_V)), MLA_KV_RANK ** -0.5)
    mla_w_out = nrm((N_MLA_LAYERS, MLA_HEADS * MLA_V, D_MODEL), (MLA_HEADS * MLA_V) ** -0.5 * DN_BETA)
    ffn_w_up = nrm((DEPTH, D_MODEL, 2 * D_FF), D_MODEL ** -0.5)
    ffn_conv_w = nrm((DEPTH, CONV_W, 2 * D_FF), CONV_W ** -0.5)
    ffn_conv_b = nrm((DEPTH, 2 * D_FF), 0.01)
    ffn_w_down = nrm((DEPTH, D_FF, D_MODEL), D_FF ** -0.5 * DN_BETA)
    ln1_g = 1.0 + nrm((DEPTH, D_MODEL), 0.02)
    ln1_b = nrm((DEPTH, D_MODEL), 0.01)
    ln2_g = 1.0 + nrm((DEPTH, D_MODEL), 0.02)
    ln2_b = nrm((DEPTH, D_MODEL), 0.01)
    return {'x_prompt': x_prompt, 'x_sample': x_sample,
            'ml_w_in': ml_w_in, 'ml_b_gates': ml_b_gates, 'ml_norm_g': ml_norm_g, 'ml_w_out': ml_w_out,
            'mla_w_in': mla_w_in, 'mla_q_norm_g': mla_q_norm_g, 'mla_kv_norm_g': mla_kv_norm_g,
            'mla_w_uq': mla_w_uq, 'mla_w_ukv': mla_w_ukv, 'mla_w_out': mla_w_out,
            'ffn_w_up': ffn_w_up, 'ffn_conv_w': ffn_conv_w, 'ffn_conv_b': ffn_conv_b, 'ffn_w_down': ffn_w_down,
            'ln1_g': ln1_g, 'ln1_b': ln1_b, 'ln2_g': ln2_g, 'ln2_b': ln2_b}


def reference(x_prompt, x_sample, ml_w_in, ml_b_gates, ml_norm_g, ml_w_out, mla_w_in, mla_q_norm_g,
              mla_kv_norm_g, mla_w_uq, mla_w_ukv, mla_w_out, ffn_w_up, ffn_conv_w, ffn_conv_b, ffn_w_down,
              ln1_g, ln1_b, ln2_g, ln2_b):
    y_prompt = trunk(x_prompt, ml_w_in, ml_b_gates, ml_norm_g, ml_w_out, mla_w_in, mla_q_norm_g, mla_kv_norm_g,
                     mla_w_uq, mla_w_ukv, mla_w_out, ffn_w_up, ffn_conv_w, ffn_conv_b, ffn_w_down,
                     ln1_g, ln1_b, ln2_g, ln2_b)
    y_sample = trunk(x_sample, ml_w_in, ml_b_gates, ml_norm_g, ml_w_out, mla_w_in, mla_q_norm_g, mla_kv_norm_g,
                     mla_w_uq, mla_w_ukv, mla_w_out, ffn_w_up, ffn_conv_w, ffn_conv_b, ffn_w_down,
                     ln1_g, ln1_b, ln2_g, ln2_b)
    return (y_prompt, y_sample)
```

```python
import functools
from typing import NamedTuple

import jax
import jax.numpy as jnp
from jax import lax
from jax.experimental import pallas as pl
from jax.experimental.pallas import tpu as pltpu

F32 = jnp.float32
BF16 = jnp.bfloat16

LANE = 128
MXU_DIM = 256
VMEM_LIMIT = 56 * 1024 * 1024


class Cfg(NamedTuple):
    d_model: int = 4096
    depth: int = 4
    ml_heads: int = 8
    ml_dqk: int = 256
    ml_dv: int = 512
    ml_chunk: int = 256
    ml_m_init: float = -1e30
    mla_heads: int = 32
    nope: int = 128
    rope: int = 64
    vdim: int = 128
    q_rank: int = 1024
    kv_rank: int = 512
    rope_base: float = 10000.0
    d_ff: int = 11008
    d_ff_pad: int = 11264
    ln_eps: float = 1e-5
    rms_eps: float = 1e-6
    tm: int = 1024
    tn: int = 1024
    tk: int = 512
    t_row: int = 256
    t_conv: int = 512
    t_ffc: int = 1024
    tq: int = 1024
    tkv: int = 1024
    kv_heads_per_step: int = 4

    @property
    def dn_alpha(self):
        return (2 * self.depth) ** 0.25

    @property
    def qk_pad(self):
        return 2 * LANE


def _params(sem, vmem=VMEM_LIMIT):
    return pltpu.CompilerParams(dimension_semantics=sem, vmem_limit_bytes=vmem)


def _mm_body(*refs, nk, n_extra, epilogue):
    x_ref, w_ref = refs[0], refs[1]
    extra = refs[2:2 + n_extra]
    outs = refs[2 + n_extra:-1]
    acc_ref = refs[-1]
    j, k = pl.program_id(1), pl.program_id(2)

    @pl.when(k == 0)
    def _():
        acc_ref[...] = jnp.zeros_like(acc_ref)

    acc_ref[...] += jnp.dot(x_ref[...], w_ref[...], preferred_element_type=F32)

    @pl.when(k == nk - 1)
    def _():
        epilogue(acc_ref, extra, outs, j)


def _matmul(x, w, *, tm, tn, tk, epilogue, out_shape, out_specs, extra=(), extra_specs=(), name):
    m, kdim = x.shape
    n = w.shape[1]
    tm, tn, tk = min(tm, m), min(tn, n), min(tk, kdim)
    assert m % tm == 0 and n % tn == 0 and kdim % tk == 0, (x.shape, w.shape, tm, tn, tk)
    nk = kdim // tk
    return pl.pallas_call(
        functools.partial(_mm_body, nk=nk, n_extra=len(extra), epilogue=epilogue),
        grid=(m // tm, n // tn, nk),
        in_specs=[pl.BlockSpec((tm, tk), lambda i, j, k: (i, k)),
                  pl.BlockSpec((tk, tn), lambda i, j, k: (k, j)),
                  *extra_specs],
        out_specs=out_specs,
        out_shape=out_shape,
        scratch_shapes=[pltpu.VMEM((tm, tn), F32)],
        compiler_params=_params(("parallel", "parallel", "arbitrary")),
        name=name,
    )(x, w, *extra)


def _ep_store(acc_ref, extra, outs, j):
    outs[0][...] = acc_ref[...].astype(outs[0].dtype)


def _ep_residual(acc_ref, extra, outs, j, *, alpha):
    outs[0][...] = alpha * extra[0][...] + acc_ref[...]


def _ep_bias(acc_ref, extra, outs, j):
    outs[0][...] = acc_ref[...] + extra[0][...]


def _ep_sigmoid_from(acc_ref, extra, outs, j, *, first_gate_tile):
    @pl.when(j < first_gate_tile)
    def _():
        outs[0][...] = acc_ref[...].astype(outs[0].dtype)

    @pl.when(j >= first_gate_tile)
    def _():
        outs[0][...] = jax.nn.sigmoid(acc_ref[...]).astype(outs[0].dtype)


def _mm_plain(x, w, cfg, out_dtype, name, tn=None):
    m, n = x.shape[0], w.shape[1]
    tm, tn = min(cfg.tm, m), min(tn or cfg.tn, n)
    return _matmul(x, w, tm=tm, tn=tn, tk=cfg.tk, epilogue=_ep_store,
                   out_shape=jax.ShapeDtypeStruct((m, n), out_dtype),
                   out_specs=pl.BlockSpec((tm, tn), lambda i, j, k: (i, j)), name=name)


def _mm_residual(x, w, resid, cfg, name):
    m, n = x.shape[0], w.shape[1]
    tm, tn = min(cfg.tm, m), min(cfg.tn, n)
    spec = pl.BlockSpec((tm, tn), lambda i, j, k: (i, j))
    return _matmul(x, w, tm=tm, tn=tn, tk=cfg.tk,
                   epilogue=functools.partial(_ep_residual, alpha=cfg.dn_alpha),
                   out_shape=jax.ShapeDtypeStruct((m, n), F32), out_specs=spec,
                   extra=(resid,), extra_specs=(spec,), name=name)


def _ln_body(y_ref, g_ref, b_ref, xf_ref, xb_ref, *, eps):
    y = y_ref[...]
    mu = jnp.mean(y, axis=-1, keepdims=True)
    d = y - mu
    var = jnp.mean(d * d, axis=-1, keepdims=True)
    out = d * lax.rsqrt(var + eps) * g_ref[...] + b_ref[...]
    xf_ref[...] = out
    xb_ref[...] = out.astype(BF16)


def _layer_norm(y, g, b, cfg):
    t, d = y.shape
    tr = min(cfg.t_row, t)
    row = pl.BlockSpec((tr, d), lambda i: (i, 0))
    vec = pl.BlockSpec((1, d), lambda i: (0, 0))
    return pl.pallas_call(
        functools.partial(_ln_body, eps=cfg.ln_eps),
        grid=(t // tr,),
        in_specs=[row, vec, vec],
        out_specs=[row, row],
        out_shape=[jax.ShapeDtypeStruct((t, d), F32), jax.ShapeDtypeStruct((t, d), BF16)],
        compiler_params=_params(("parallel",)),
        name="layer_norm",
    )(y, g.reshape(1, d), b.reshape(1, d))


HALO_ROWS = 16


def _conv_gate_body(hg_ref, hgp_ref, hgn_ref, hu_ref, hup_ref, hun_ref,
                    cwg_ref, cbg_ref, cwu_ref, cbu_ref, out_ref, *, tr, seq_starts, seq_ends):
    i = pl.program_id(0)
    r0 = i * tr
    at_start = functools.reduce(jnp.logical_or, [r0 == s for s in seq_starts])
    at_end = functools.reduce(jnp.logical_or, [r0 + tr == e for e in seq_ends])
    rows = lax.broadcasted_iota(jnp.int32, (tr, 1), 0)

    def conv(h_ref, hp_ref, hn_ref, cw_ref, cb_ref):
        h = h_ref[...].astype(F32)
        prev_row = jnp.where(at_start, 0.0, hp_ref[HALO_ROWS - 1:HALO_ROWS, :].astype(F32))
        next_row = jnp.where(at_end, 0.0, hn_ref[0:1, :].astype(F32))
        h_prev = jnp.where(rows == 0, prev_row, pltpu.roll(h, 1, 0))
        h_next = jnp.where(rows == tr - 1, next_row, pltpu.roll(h, tr - 1, 0))
        cw = cw_ref[...]
        return h_prev * cw[0:1, :] + h * cw[1:2, :] + h_next * cw[2:3, :] + cb_ref[...]

    g = conv(hg_ref, hgp_ref, hgn_ref, cwg_ref, cbg_ref)
    u = conv(hu_ref, hup_ref, hun_ref, cwu_ref, cbu_ref)
    out_ref[...] = (g * jax.nn.sigmoid(g) * u).astype(out_ref.dtype)


def _conv_gate(h, conv_w, conv_b, seqs, cfg):
    t = h.shape[0]
    fp = h.shape[1] // 2
    tr, tc = min(cfg.t_conv, t), min(cfg.t_ffc, fp)
    assert t % tr == 0 and fp % tc == 0 and tr % HALO_ROWS == 0
    nj = fp // tc
    hb = tr // HALO_ROWS
    last = t // HALO_ROWS - 1

    def main(off):
        return pl.BlockSpec((tr, tc), lambda i, j: (i, j + off))

    def prev(off):
        return pl.BlockSpec((HALO_ROWS, tc), lambda i, j: (jnp.maximum(i * hb - 1, 0), j + off))

    def nxt(off):
        return pl.BlockSpec((HALO_ROWS, tc), lambda i, j: (jnp.minimum((i + 1) * hb, last), j + off))

    def cw(off):
        return pl.BlockSpec((3, tc), lambda i, j: (0, j + off))

    def cb(off):
        return pl.BlockSpec((1, tc), lambda i, j: (0, j + off))

    return pl.pallas_call(
        functools.partial(_conv_gate_body, tr=tr, seq_starts=tuple(s for s, _ in seqs),
                          seq_ends=tuple(s + n for s, n in seqs)),
        grid=(t // tr, nj),
        in_specs=[main(0), prev(0), nxt(0), main(nj), prev(nj), nxt(nj), cw(0), cb(0), cw(nj), cb(nj)],
        out_specs=pl.BlockSpec((tr, tc), lambda i, j: (i, j)),
        out_shape=jax.ShapeDtypeStruct((t, fp), BF16),
        compiler_params=_params(("parallel", "parallel")),
        name="conv_gate",
    )(h, h, h, h, h, h, conv_w, conv_b, conv_w, conv_b)


def _mlstm_body(q_ref, k_ref, v_ref, g_ref, h_ref, c_scr, n_scr, m_scr, *,
                chunk, nc, reverse, reset_chunks, m_init):
    c = pl.program_id(1)
    cc = nc - 1 - c if reverse else c

    @pl.when(functools.reduce(jnp.logical_or, [cc == r for r in reset_chunks]))
    def _():
        c_scr[...] = jnp.zeros_like(c_scr)
        n_scr[...] = jnp.zeros_like(n_scr)
        m_scr[...] = jnp.full_like(m_scr, m_init)

    gates = g_ref[0]
    ri, rf = (2, 3) if reverse else (0, 1)
    i_row = gates[ri:ri + 1, :]
    lf_row = jax.nn.log_sigmoid(gates[rf:rf + 1, :])

    row = lax.broadcasted_iota(jnp.int32, (chunk, chunk), 0)
    col = lax.broadcasted_iota(jnp.int32, (chunk, chunk), 1)
    visible = (col >= row) if reverse else (col <= row)
    visible_t = (row >= col) if reverse else (row <= col)
    b_col = jnp.sum(jnp.where(visible, lf_row, 0.0), axis=-1, keepdims=True)
    lf_col = jnp.sum(jnp.where(row == col, lf_row, 0.0), axis=-1, keepdims=True)
    b_row = jnp.sum(jnp.where(visible_t, lf_col, 0.0), axis=0, keepdims=True)

    m_prev = m_scr[0:1, 0:1]
    logd = jnp.where(visible, b_col - b_row + i_row, -jnp.inf)
    inter = b_col + m_prev
    m_row = jnp.maximum(jnp.max(logd, axis=-1, keepdims=True), inter)
    dmat = jnp.exp(logd - m_row)

    q = q_ref[...]
    k = k_ref[...]
    v = v_ref[...]
    s = lax.dot_general(q, k, (((1,), (1,)), ((), ())), preferred_element_type=F32) * dmat
    w_inter = jnp.exp(inter - m_row)
    ct = c_scr[...]
    qc = jnp.dot(q, ct.astype(BF16), preferred_element_type=F32)
    num = jnp.dot(s.astype(BF16), v, preferred_element_type=F32) + w_inter * qc
    qn = jnp.dot(q, n_scr[...].astype(BF16), preferred_element_type=F32)[:, 0:1]
    den = jnp.sum(s, axis=-1, keepdims=True) + w_inter * qn
    h_ref[...] = num / jnp.maximum(jnp.abs(den), jnp.exp(-m_row))

    bl = jnp.sum(lf_row, axis=-1, keepdims=True)
    a_row = bl - b_row + i_row
    m_new = jnp.maximum(bl + m_prev, jnp.max(a_row, axis=-1, keepdims=True))
    decay = jnp.exp(bl + m_prev - m_new)
    wk_t = k.astype(F32).T * jnp.exp(a_row - m_new)
    c_scr[...] = decay * ct + jnp.dot(wk_t.astype(BF16), v, preferred_element_type=F32)
    n_scr[...] = decay * n_scr[...] + jnp.sum(wk_t, axis=-1, keepdims=True)
    m_scr[...] = jnp.broadcast_to(m_new, m_scr.shape)


def _mlstm_scan(qkv, gates, seqs, cfg, *, reverse):
    t = qkv.shape[0]
    nh, dqk, dv, chunk = cfg.ml_heads, cfg.ml_dqk, cfg.ml_dv, cfg.ml_chunk
    chunk = min(chunk, min(n for _, n in seqs))
    assert all(s % chunk == 0 and n % chunk == 0 for s, n in seqs)
    nc = t // chunk
    if reverse:
        resets = tuple((s + n) // chunk - 1 for s, n in seqs)
    else:
        resets = tuple(s // chunk for s, _ in seqs)

    def ci(c):
        return nc - 1 - c if reverse else c

    v_off = 2 * nh * dqk // dv
    assert (2 * nh * dqk) % dv == 0
    return pl.pallas_call(
        functools.partial(_mlstm_body, chunk=chunk, nc=nc, reverse=reverse, reset_chunks=resets,
                          m_init=cfg.ml_m_init),
        grid=(nh, nc),
        in_specs=[pl.BlockSpec((chunk, dqk), lambda h, c: (ci(c), h)),
                  pl.BlockSpec((chunk, dqk), lambda h, c: (ci(c), nh + h)),
                  pl.BlockSpec((chunk, dv), lambda h, c: (ci(c), v_off + h)),
                  pl.BlockSpec((1, 8, chunk), lambda h, c: (h, 0, ci(c)))],
        out_specs=pl.BlockSpec((chunk, dv), lambda h, c: (ci(c), h)),
        out_shape=jax.ShapeDtypeStruct((t, nh * dv), F32),
        scratch_shapes=[pltpu.VMEM((dqk, dv), F32), pltpu.VMEM((dqk, LANE), F32), pltpu.VMEM((8, LANE), F32)],
        compiler_params=_params(("parallel", "arbitrary")),
        name="mlstm_scan_bw" if reverse else "mlstm_scan_fw",
    )(qkv, qkv, qkv, gates)


def _mlstm_combine_body(hf_ref, hb_ref, o_ref, g_ref, out_ref, *, nh, dv, eps):
    for hh in range(nh):
        sl = slice(hh * dv, (hh + 1) * dv)
        h = hf_ref[:, sl] + hb_ref[:, sl]
        h = h * lax.rsqrt(jnp.mean(h * h, axis=-1, keepdims=True) + eps)
        out_ref[:, sl] = (h * g_ref[:, sl] * o_ref[:, sl].astype(F32)).astype(out_ref.dtype)


def _mlstm_combine(h_fw, h_bw, proj, norm_g, cfg):
    t, w = h_fw.shape
    tr = min(cfg.t_row, t)
    o_off = (2 * cfg.ml_heads * cfg.ml_dqk + w) // w
    assert (2 * cfg.ml_heads * cfg.ml_dqk) % w == 0
    row = pl.BlockSpec((tr, w), lambda i: (i, 0))
    return pl.pallas_call(
        functools.partial(_mlstm_combine_body, nh=cfg.ml_heads, dv=cfg.ml_dv, eps=cfg.rms_eps),
        grid=(t // tr,),
        in_specs=[row, row, pl.BlockSpec((tr, w), lambda i: (i, o_off)), pl.BlockSpec((1, w), lambda i: (0, 0))],
        out_specs=row,
        out_shape=jax.ShapeDtypeStruct((t, w), BF16),
        compiler_params=_params(("parallel",)),
        name="mlstm_combine",
    )(h_fw, h_bw, proj, norm_g.reshape(1, w))


def _mlstm_mixer(xf, xb, w_in, b_gates, norm_g, w_out, seqs, cfg):
    t = xf.shape[0]
    nh, dqk, dv = cfg.ml_heads, cfg.ml_dqk, cfg.ml_dv
    qk_w, v_w = nh * dqk, nh * dv
    main_w = 2 * qk_w + 2 * v_w
    n_gates = 4 * nh
    col_scale = jnp.concatenate([jnp.ones((qk_w,), F32), jnp.full((qk_w,), dqk ** -0.5, F32),
                                 jnp.ones((2 * v_w,), F32)])
    w_main = (w_in[:, :main_w] * col_scale).astype(BF16)
    w_gate = jnp.pad(w_in[:, main_w:], ((0, 0), (0, LANE - n_gates))).astype(BF16)
    b_gate = jnp.pad(b_gates, (0, LANE - n_gates)).reshape(1, LANE)

    tm, tn = min(cfg.tm, t), min(cfg.tn, v_w)
    assert (2 * qk_w + v_w) % tn == 0
    proj = _matmul(xb, w_main, tm=tm, tn=tn, tk=cfg.tk,
                   epilogue=functools.partial(_ep_sigmoid_from, first_gate_tile=(2 * qk_w + v_w) // tn),
                   out_shape=jax.ShapeDtypeStruct((t, main_w), BF16),
                   out_specs=pl.BlockSpec((tm, tn), lambda i, j, k: (i, j)), name="mlstm_in_proj")
    gates = _matmul(xb, w_gate, tm=tm, tn=LANE, tk=cfg.tk, epilogue=_ep_bias,
                    out_shape=jax.ShapeDtypeStruct((t, LANE), F32),
                    out_specs=pl.BlockSpec((tm, LANE), lambda i, j, k: (i, 0)),
                    extra=(b_gate,), extra_specs=(pl.BlockSpec((1, LANE), lambda i, j, k: (0, 0)),),
                    name="mlstm_gate_proj")
    g = gates[:, :n_gates].reshape(t, 4, nh).transpose(2, 1, 0)
    g = jnp.pad(g, ((0, 0), (0, 4), (0, 0)))
    h_fw = _mlstm_scan(proj, g, seqs, cfg, reverse=False)
    h_bw = _mlstm_scan(proj, g, seqs, cfg, reverse=True)
    hact = _mlstm_combine(h_fw, h_bw, proj, norm_g, cfg)
    return _mm_residual(hact, w_out.astype(BF16), xf, cfg, "mlstm_out_proj")


def _rope_tables(max_len, cfg, scale):
    half = cfg.rope // 2
    inv_freq = jnp.power(cfg.rope_base, -jnp.arange(0, cfg.rope, 2, dtype=F32) / cfg.rope)
    ang = jnp.arange(max_len, dtype=F32)[:, None] * inv_freq[None, :]
    cos, sin = jnp.cos(ang), jnp.sin(ang)
    z = jnp.zeros((max_len, LANE - 2 * half), F32)
    zh = jnp.zeros((max_len, half), F32)
    cos_t = jnp.concatenate([cos, cos, z], axis=1) * scale
    sin_a = jnp.concatenate([-sin, zh, z], axis=1) * scale
    sin_b = jnp.concatenate([zh, sin, z], axis=1) * scale
    return cos_t, sin_a, sin_b


def _rotate(x, cos_t, sin_a, sin_b, half):
    return x * cos_t + pltpu.roll(x, LANE - half, 1) * sin_a + pltpu.roll(x, half, 1) * sin_b


def _ep_mla_in(acc_ref, extra, outs, j, *, q_rank, kv_rank, eps):
    gq_ref, gkv_ref = extra
    cq_ref, ckv_ref, kr_ref = outs

    def rms(x, g):
        return x * lax.rsqrt(jnp.mean(x * x, axis=-1, keepdims=True) + eps) * g

    cq_ref[...] = rms(acc_ref[:, :q_rank], gq_ref[...]).astype(cq_ref.dtype)
    ckv_ref[...] = rms(acc_ref[:, q_rank:q_rank + kv_rank], gkv_ref[...]).astype(ckv_ref.dtype)
    kr_ref[...] = acc_ref[:, q_rank + kv_rank:]


def _ep_q_up(acc_ref, extra, outs, j, *, heads, scale, half):
    cos_ref, sa_ref, sb_ref = extra
    out = outs[0]
    for hh in range(heads):
        lo = hh * 2 * LANE
        out[:, lo:lo + LANE] = (acc_ref[:, lo:lo + LANE] * scale).astype(out.dtype)
        r = acc_ref[:, lo + LANE:lo + 2 * LANE]
        out[:, lo + LANE:lo + 2 * LANE] = _rotate(r, cos_ref[...], sa_ref[...], sb_ref[...], half).astype(out.dtype)


def _kv_up_body(x_ref, wk_ref, wv_ref, kr_ref, cos_ref, sa_ref, sb_ref, k_out, v_out, *, heads, half):
    x = x_ref[...]
    kk = jnp.dot(x, wk_ref[...], preferred_element_type=F32)
    v_out[...] = jnp.dot(x, wv_ref[...], preferred_element_type=F32).astype(v_out.dtype)
    kr = _rotate(kr_ref[...], cos_ref[...], sa_ref[...], sb_ref[...], half).astype(k_out.dtype)
    for hh in range(heads):
        lo = hh * 2 * LANE
        k_out[:, lo:lo + LANE] = kk[:, hh * LANE:(hh + 1) * LANE].astype(k_out.dtype)
        k_out[:, lo + LANE:lo + 2 * LANE] = kr


def _pos_block(seqs, tm):
    def f(i):
        out = i
        for s, _ in seqs:
            out = jnp.where(i >= s // tm, i - s // tm, out)
        return out
    return f


def _attn_body(q_ref, k_ref, v_ref, o_ref, m_scr, l_scr, acc_scr, *, nkv):
    ki = pl.program_id(3)

    @pl.when(ki == 0)
    def _():
        m_scr[...] = jnp.full_like(m_scr, -jnp.inf)
        l_scr[...] = jnp.zeros_like(l_scr)
        acc_scr[...] = jnp.zeros_like(acc_scr)

    s = lax.dot_general(q_ref[...], k_ref[...], (((1,), (1,)), ((), ())), preferred_element_type=F32)
    m_prev = m_scr[...]
    m_new = jnp.maximum(m_prev, jnp.max(s, axis=-1, keepdims=True))
    alpha = jnp.exp(m_prev - m_new)
    p = jnp.exp(s - m_new)
    l_scr[...] = alpha * l_scr[...] + jnp.sum(p, axis=-1, keepdims=True)
    acc_scr[...] = alpha * acc_scr[...] + jnp.dot(p.astype(v_ref.dtype), v_ref[...], preferred_element_type=F32)
    m_scr[...] = m_new

    @pl.when(ki == nkv - 1)
    def _():
        o_ref[...] = (acc_scr[...] / l_scr[...]).astype(o_ref.dtype)


def _attention(q, k, v, row0, nseq, slen, cfg):
    t = q.shape[0]
    nh, vd, qw = cfg.mla_heads, cfg.vdim, cfg.qk_pad
    tq, tkv = min(cfg.tq, slen), min(cfg.tkv, slen)
    assert slen % tq == 0 and slen % tkv == 0 and row0 % tq == 0 and row0 % tkv == 0
    nq, nkv = slen // tq, slen // tkv
    q0, k0 = row0 // tq, row0 // tkv
    return pl.pallas_call(
        functools.partial(_attn_body, nkv=nkv),
        grid=(nseq, nh, nq, nkv),
        in_specs=[pl.BlockSpec((tq, qw), lambda b, h, qi, ki: (q0 + b * nq + qi, h)),
                  pl.BlockSpec((tkv, qw), lambda b, h, qi, ki: (k0 + b * nkv + ki, h)),
                  pl.BlockSpec((tkv, vd), lambda b, h, qi, ki: (k0 + b * nkv + ki, h))],
        out_specs=pl.BlockSpec((tq, vd), lambda b, h, qi, ki: (b * nq + qi, h)),
        out_shape=jax.ShapeDtypeStruct((nseq * slen, nh * vd), BF16),
        scratch_shapes=[pltpu.VMEM((tq, 1), F32), pltpu.VMEM((tq, 1), F32), pltpu.VMEM((tq, vd), F32)],
        compiler_params=_params(("parallel", "parallel", "parallel", "arbitrary")),
        name="mla_attention",
    )(q, k, v)


def _mla_mixer(xf, xb, w_in, q_norm_g, kv_norm_g, w_uq, w_ukv, w_out, seqs, cfg):
    t, d = xf.shape
    nh, nope, rope, vd = cfg.mla_heads, cfg.nope, cfg.rope, cfg.vdim
    qr, kvr = cfg.q_rank, cfg.kv_rank
    half = rope // 2
    assert nope == LANE and vd == LANE and rope <= LANE
    qw = cfg.qk_pad
    tm = min(cfg.tm, t)
    max_len = max(n for _, n in seqs)
    pos = _pos_block(seqs, tm)

    w_in_p = jnp.pad(w_in, ((0, 0), (0, LANE - rope))).astype(BF16)
    w_uq_p = jnp.pad(w_uq.reshape(qr, nh, nope + rope), ((0, 0), (0, 0), (0, qw - nope - rope)))
    w_uq_p = w_uq_p.reshape(qr, nh * qw).astype(BF16)
    w_ukv_h = w_ukv.reshape(kvr, nh, nope + vd)
    w_uk = w_ukv_h[:, :, :nope].reshape(kvr, nh * nope).astype(BF16)
    w_uv = w_ukv_h[:, :, nope:].reshape(kvr, nh * vd).astype(BF16)

    n_in = qr + kvr + LANE
    cq, ckv, kr = _matmul(
        xb, w_in_p, tm=tm, tn=n_in, tk=cfg.tk,
        epilogue=functools.partial(_ep_mla_in, q_rank=qr, kv_rank=kvr, eps=cfg.rms_eps),
        out_shape=[jax.ShapeDtypeStruct((t, qr), BF16), jax.ShapeDtypeStruct((t, kvr), BF16),
                   jax.ShapeDtypeStruct((t, LANE), F32)],
        out_specs=[pl.BlockSpec((tm, qr), lambda i, j, k: (i, 0)),
                   pl.BlockSpec((tm, kvr), lambda i, j, k: (i, 0)),
                   pl.BlockSpec((tm, LANE), lambda i, j, k: (i, 0))],
        extra=(q_norm_g.reshape(1, qr), kv_norm_g.reshape(1, kvr)),
        extra_specs=(pl.BlockSpec((1, qr), lambda i, j, k: (0, 0)),
                     pl.BlockSpec((1, kvr), lambda i, j, k: (0, 0))),
        name="mla_in_proj")

    scale = (nope + rope) ** -0.5
    q_tabs = _rope_tables(max_len, cfg, scale)
    k_tabs = _rope_tables(max_len, cfg, 1.0)
    tn = min(cfg.tn, nh * qw)
    tab3 = pl.BlockSpec((tm, LANE), lambda i, j, k: (pos(i), 0))
    q = _matmul(cq, w_uq_p, tm=tm, tn=tn, tk=cfg.tk,
                epilogue=functools.partial(_ep_q_up, heads=tn // qw, scale=scale, half=half),
                out_shape=jax.ShapeDtypeStruct((t, nh * qw), BF16),
                out_specs=pl.BlockSpec((tm, tn), lambda i, j, k: (i, j)),
                extra=q_tabs, extra_specs=(tab3, tab3, tab3), name="mla_q_up")

    hb = min(cfg.kv_heads_per_step, nh)
    tab2 = pl.BlockSpec((tm, LANE), lambda i, j: (pos(i), 0))
    k, v = pl.pallas_call(
        functools.partial(_kv_up_body, heads=hb, half=half),
        grid=(t // tm, nh // hb),
        in_specs=[pl.BlockSpec((tm, kvr), lambda i, j: (i, 0)),
                  pl.BlockSpec((kvr, hb * nope), lambda i, j: (0, j)),
                  pl.BlockSpec((kvr, hb * vd), lambda i, j: (0, j)),
                  pl.BlockSpec((tm, LANE), lambda i, j: (i, 0)),
                  tab2, tab2, tab2],
        out_specs=[pl.BlockSpec((tm, hb * qw), lambda i, j: (i, j)),
                   pl.BlockSpec((tm, hb * vd), lambda i, j: (i, j))],
        out_shape=[jax.ShapeDtypeStruct((t, nh * qw), BF16), jax.ShapeDtypeStruct((t, nh * vd), BF16)],
        compiler_params=_params(("parallel", "parallel")),
        name="mla_kv_up",
    )(ckv, w_uk, w_uv, kr, *k_tabs)

    outs = []
    idx = 0
    while idx < len(seqs):
        row0, slen = seqs[idx]
        nseq = 1
        while idx + nseq < len(seqs) and seqs[idx + nseq] == (row0 + nseq * slen, slen):
            nseq += 1
        outs.append(_attention(q, k, v, row0, nseq, slen, cfg))
        idx += nseq
    o = outs[0] if len(outs) == 1 else jnp.concatenate(outs, axis=0)
    return _mm_residual(o, w_out.astype(BF16), xf, cfg, "mla_out_proj")


def _conv_ffn(xf, xb, w_up, conv_w, conv_b, w_down, seqs, cfg):
    f, fp = cfg.d_ff, cfg.d_ff_pad
    pad = fp - f

    def pad_halves(a):
        lead = [(0, 0)] * (a.ndim - 1)
        return jnp.concatenate([jnp.pad(a[..., :f], lead + [(0, pad)]), jnp.pad(a[..., f:], lead + [(0, pad)])], -1)

    w_up_p = pad_halves(w_up).astype(BF16)
    w_down_p = jnp.pad(w_down, ((0, pad), (0, 0))).astype(BF16)
    h = _mm_plain(xb, w_up_p, cfg, BF16, "ffn_up")
    act = _conv_gate(h, pad_halves(conv_w), pad_halves(conv_b.reshape(1, -1)), seqs, cfg)
    return _mm_residual(act, w_down_p, xf, cfg, "ffn_down")


def _trunk(x, seqs, cfg, ml_w_in, ml_b_gates, ml_norm_g, ml_w_out, mla_w_in, mla_q_norm_g, mla_kv_norm_g,
           mla_w_uq, mla_w_ukv, mla_w_out, ffn_w_up, ffn_conv_w, ffn_conv_b, ffn_w_down,
           ln1_g, ln1_b, ln2_g, ln2_b):
    xf, xb = x, x.astype(BF16)
    for i in range(cfg.depth):
        j = i // 2
        if i % 2 == 0:
            y = _mlstm_mixer(xf, xb, ml_w_in[j], ml_b_gates[j], ml_norm_g[j], ml_w_out[j], seqs, cfg)
        else:
            y = _mla_mixer(xf, xb, mla_w_in[j], mla_q_norm_g[j], mla_kv_norm_g[j],
                           mla_w_uq[j], mla_w_ukv[j], mla_w_out[j], seqs, cfg)
        xf, xb = _layer_norm(y, ln1_g[i], ln1_b[i], cfg)
        y = _conv_ffn(xf, xb, ffn_w_up[i], ffn_conv_w[i], ffn_conv_b[i], ffn_w_down[i], seqs, cfg)
        xf, xb = _layer_norm(y, ln2_g[i], ln2_b[i], cfg)
    return xf


def _run(cfg, x_prompt, x_sample, *weights):
    d = x_prompt.shape[-1]
    shapes = [x_prompt.shape, x_sample.shape]
    seqs, row = [], 0
    for b, s, _ in shapes:
        for _ in range(b):
            seqs.append((row, s))
            row += s
    x = jnp.concatenate([x_prompt.reshape(-1, d), x_sample.reshape(-1, d)], axis=0)
    y = _trunk(x, tuple(seqs), cfg, *weights)
    n_prompt = shapes[0][0] * shapes[0][1]
    return y[:n_prompt].reshape(shapes[0]), y[n_prompt:].reshape(shapes[1])


def kernel(x_prompt, x_sample, ml_w_in, ml_b_gates, ml_norm_g, ml_w_out, mla_w_in, mla_q_norm_g, mla_kv_norm_g, mla_w_uq, mla_w_ukv, mla_w_out, ffn_w_up, ffn_conv_w, ffn_conv_b, ffn_w_down, ln1_g, ln1_b, ln2_g, ln2_b):
    return _run(Cfg(), x_prompt, x_sample, ml_w_in, ml_b_gates, ml_norm_g, ml_w_out, mla_w_in, mla_q_norm_g,
                mla_kv_norm_g, mla_w_uq, mla_w_ukv, mla_w_out, ffn_w_up, ffn_conv_w, ffn_conv_b, ffn_w_down,
                ln1_g, ln1_b, ln2_g, ln2_b)
```

```python
import functools
from typing import NamedTuple

import jax
import jax.numpy as jnp
from jax import lax
from jax.experimental import pallas as pl
from jax.experimental.pallas import tpu as pltpu

F32 = jnp.float32
BF16 = jnp.bfloat16

LANE = 128
MXU_DIM = 256
LOG2_E = 1.4426950408889634
VMEM_LIMIT = 56 * 1024 * 1024


class Cfg(NamedTuple):
    d_model: int = 4096
    depth: int = 4
    ml_heads: int = 8
    ml_dqk: int = 256
    ml_dv: int = 512
    ml_chunk: int = 256
    ml_m_init: float = -1e30
    mla_heads: int = 32
    nope: int = 128
    rope: int = 64
    vdim: int = 128
    q_rank: int = 1024
    kv_rank: int = 512
    rope_base: float = 10000.0
    d_ff: int = 11008
    d_ff_pad: int = 11264
    ln_eps: float = 1e-5
    rms_eps: float = 1e-6
    mm_wide: tuple = (1024, 512, 4096)
    mm_down: tuple = (1024, 1024, 2816)
    mm_mla_in: tuple = (1024, 0, 512)
    mm_q_up: tuple = (1024, 1024, 1024)
    t_row: int = 256
    t_conv: int = 512
    t_ffc: int = 1024
    tq: int = 1024
    tkv: int = 1024
    attn_heads_per_step: int = 8
    kv_heads_per_step: int = 4

    @property
    def dn_alpha(self):
        return (2 * self.depth) ** 0.25

    @property
    def qk_pad(self):
        return 2 * LANE


def _params(sem, vmem=VMEM_LIMIT):
    return pltpu.CompilerParams(dimension_semantics=sem, vmem_limit_bytes=vmem)


def _mm_body(*refs, nk, n_extra, epilogue):
    x_ref, w_ref = refs[0], refs[1]
    extra = refs[2:2 + n_extra]
    j, k = pl.program_id(1), pl.program_id(2)
    part = jnp.dot(x_ref[...], w_ref[...], preferred_element_type=F32)
    if nk == 1:
        epilogue(part, extra, refs[2 + n_extra:], j)
        return
    outs, acc_ref = refs[2 + n_extra:-1], refs[-1]

    @pl.when(k == 0)
    def _():
        acc_ref[...] = part

    @pl.when(jnp.logical_and(k > 0, k < nk - 1))
    def _():
        acc_ref[...] += part

    @pl.when(k == nk - 1)
    def _():
        epilogue(acc_ref[...] + part, extra, outs, j)


def _matmul(x, w, *, tm, tn, tk, epilogue, out_shape, out_specs, extra=(), extra_specs=(), name):
    m, kdim = x.shape
    n = w.shape[1]
    tm, tn, tk = min(tm, m), min(tn, n), min(tk, kdim)
    assert m % tm == 0 and n % tn == 0 and kdim % tk == 0, (x.shape, w.shape, tm, tn, tk)
    nk = kdim // tk
    return pl.pallas_call(
        functools.partial(_mm_body, nk=nk, n_extra=len(extra), epilogue=epilogue),
        grid=(m // tm, n // tn, nk),
        in_specs=[pl.BlockSpec((tm, tk), lambda i, j, k: (i, k)),
                  pl.BlockSpec((tk, tn), lambda i, j, k: (k, j)),
                  *extra_specs],
        out_specs=out_specs,
        out_shape=out_shape,
        scratch_shapes=[] if nk == 1 else [pltpu.VMEM((tm, tn), F32)],
        compiler_params=_params(("parallel", "parallel", "arbitrary")),
        name=name,
    )(x, w, *extra)


def _ep_store(acc, extra, outs, j):
    outs[0][...] = acc.astype(outs[0].dtype)


def _ep_residual(acc, extra, outs, j, *, alpha):
    outs[0][...] = alpha * extra[0][...] + acc


def _ep_bias(acc, extra, outs, j):
    outs[0][...] = acc + extra[0][...]


def _ep_sigmoid_from(acc, extra, outs, j, *, first_gate_tile):
    @pl.when(j < first_gate_tile)
    def _():
        outs[0][...] = acc.astype(outs[0].dtype)

    @pl.when(j >= first_gate_tile)
    def _():
        outs[0][...] = jax.nn.sigmoid(acc).astype(outs[0].dtype)


def _mm_plain(x, w, tiles, out_dtype, name):
    m, n = x.shape[0], w.shape[1]
    tm, tn, tk = tiles
    tm, tn = min(tm, m), min(tn, n)
    return _matmul(x, w, tm=tm, tn=tn, tk=tk, epilogue=_ep_store,
                   out_shape=jax.ShapeDtypeStruct((m, n), out_dtype),
                   out_specs=pl.BlockSpec((tm, tn), lambda i, j, k: (i, j)), name=name)


def _mm_residual(x, w, resid, tiles, cfg, name):
    m, n = x.shape[0], w.shape[1]
    tm, tn, tk = tiles
    tm, tn = min(tm, m), min(tn, n)
    spec = pl.BlockSpec((tm, tn), lambda i, j, k: (i, j))
    return _matmul(x, w, tm=tm, tn=tn, tk=tk,
                   epilogue=functools.partial(_ep_residual, alpha=cfg.dn_alpha),
                   out_shape=jax.ShapeDtypeStruct((m, n), F32), out_specs=spec,
                   extra=(resid,), extra_specs=(spec,), name=name)


def _ln_body(y_ref, g_ref, b_ref, xf_ref, xb_ref, *, eps):
    y = y_ref[...]
    mu = jnp.mean(y, axis=-1, keepdims=True)
    d = y - mu
    var = jnp.mean(d * d, axis=-1, keepdims=True)
    out = d * lax.rsqrt(var + eps) * g_ref[...] + b_ref[...]
    xf_ref[...] = out
    xb_ref[...] = out.astype(BF16)


def _layer_norm(y, g, b, cfg):
    t, d = y.shape
    tr = min(cfg.t_row, t)
    row = pl.BlockSpec((tr, d), lambda i: (i, 0))
    vec = pl.BlockSpec((1, d), lambda i: (0, 0))
    return pl.pallas_call(
        functools.partial(_ln_body, eps=cfg.ln_eps),
        grid=(t // tr,),
        in_specs=[row, vec, vec],
        out_specs=[row, row],
        out_shape=[jax.ShapeDtypeStruct((t, d), F32), jax.ShapeDtypeStruct((t, d), BF16)],
        compiler_params=_params(("parallel",)),
        name="layer_norm",
    )(y, g.reshape(1, d), b.reshape(1, d))


HALO_ROWS = 16

def _conv_gate_body(hg_ref, hgp_ref, hgn_ref, hu_ref, hup_ref, hun_ref,
                    cwg_ref, cbg_ref, cwu_ref, cbu_ref, out_ref, *, tr, seq_starts, seq_ends):
    i = pl.program_id(0)
    r0 = i * tr
    at_start = functools.reduce(jnp.logical_or, [r0 == s for s in seq_starts])
    at_end = functools.reduce(jnp.logical_or, [r0 + tr == e for e in seq_ends])
    rows = lax.broadcasted_iota(jnp.int32, (tr, 1), 0)

    def conv(h_ref, hp_ref, hn_ref, cw_ref, cb_ref):
        h = h_ref[...].astype(F32)
        prev_row = jnp.where(at_start, 0.0, hp_ref[HALO_ROWS - 1:HALO_ROWS, :].astype(F32))
        next_row = jnp.where(at_end, 0.0, hn_ref[0:1, :].astype(F32))
        h_prev = jnp.where(rows == 0, prev_row, pltpu.roll(h, 1, 0))
        h_next = jnp.where(rows == tr - 1, next_row, pltpu.roll(h, tr - 1, 0))
        cw = cw_ref[...]
        return h_prev * cw[0:1, :] + h * cw[1:2, :] + h_next * cw[2:3, :] + cb_ref[...]

    g = conv(hg_ref, hgp_ref, hgn_ref, cwg_ref, cbg_ref)
    u = conv(hu_ref, hup_ref, hun_ref, cwu_ref, cbu_ref)
    out_ref[...] = (g * jax.nn.sigmoid(g) * u).astype(out_ref.dtype)


def _conv_gate(h, conv_w, conv_b, seqs, cfg):
    t = h.shape[0]
    fp = h.shape[1] // 2
    tr, tc = min(cfg.t_conv, t), min(cfg.t_ffc, fp)
    assert t % tr == 0 and fp % tc == 0 and tr % HALO_ROWS == 0
    nj = fp // tc
    hb = tr // HALO_ROWS
    last = t // HALO_ROWS - 1

    def main(off):
        return pl.BlockSpec((tr, tc), lambda i, j: (i, j + off))

    def prev(off):
        return pl.BlockSpec((HALO_ROWS, tc), lambda i, j: (jnp.maximum(i * hb - 1, 0), j + off))

    def nxt(off):
        return pl.BlockSpec((HALO_ROWS, tc), lambda i, j: (jnp.minimum((i + 1) * hb, last), j + off))

    def cw(off):
        return pl.BlockSpec((3, tc), lambda i, j: (0, j + off))

    def cb(off):
        return pl.BlockSpec((1, tc), lambda i, j: (0, j + off))

    return pl.pallas_call(
        functools.partial(_conv_gate_body, tr=tr, seq_starts=tuple(s for s, _ in seqs),
                          seq_ends=tuple(s + n for s, n in seqs)),
        grid=(t // tr, nj),
        in_specs=[main(0), prev(0), nxt(0), main(nj), prev(nj), nxt(nj), cw(0), cb(0), cw(nj), cb(nj)],
        out_specs=pl.BlockSpec((tr, tc), lambda i, j: (i, j)),
        out_shape=jax.ShapeDtypeStruct((t, fp), BF16),
        compiler_params=_params(("parallel", "parallel")),
        name="conv_gate",
    )(h, h, h, h, h, h, conv_w, conv_b, conv_w, conv_b)


def _mlstm_body(q_ref, k_ref, v_ref, g_ref, h_ref, c_scr, n_scr, m_scr, *,
                chunk, nc, reverse, reset_chunks, m_init):
    c = pl.program_id(1)
    cc = nc - 1 - c if reverse else c

    @pl.when(functools.reduce(jnp.logical_or, [cc == r for r in reset_chunks]))
    def _():
        c_scr[...] = jnp.zeros_like(c_scr)
        n_scr[...] = jnp.zeros_like(n_scr)
        m_scr[...] = jnp.full_like(m_scr, m_init)

    gates = g_ref[0]
    ri, rf = (2, 3) if reverse else (0, 1)
    i_row = gates[ri:ri + 1, :]
    lf_row = jax.nn.log_sigmoid(gates[rf:rf + 1, :])

    row = lax.broadcasted_iota(jnp.int32, (chunk, chunk), 0)
    col = lax.broadcasted_iota(jnp.int32, (chunk, chunk), 1)
    visible = (col >= row) if reverse else (col <= row)
    visible_t = (row >= col) if reverse else (row <= col)
    b_col = jnp.sum(jnp.where(visible, lf_row, 0.0), axis=-1, keepdims=True)
    lf_col = jnp.sum(jnp.where(row == col, lf_row, 0.0), axis=-1, keepdims=True)
    b_row = jnp.sum(jnp.where(visible_t, lf_col, 0.0), axis=0, keepdims=True)

    m_prev = m_scr[0:1, 0:1]
    logd = jnp.where(visible, b_col - b_row + i_row, -jnp.inf)
    inter = b_col + m_prev
    m_row = jnp.maximum(jnp.max(logd, axis=-1, keepdims=True), inter)
    dmat = jnp.exp(logd - m_row)

    q = q_ref[...]
    k = k_ref[...]
    v = v_ref[...]
    s = lax.dot_general(q, k, (((1,), (1,)), ((), ())), preferred_element_type=F32) * dmat
    w_inter = jnp.exp(inter - m_row)
    ct = c_scr[...]
    qc = jnp.dot(q, ct.astype(BF16), preferred_element_type=F32)
    num = jnp.dot(s.astype(BF16), v, preferred_element_type=F32) + w_inter * qc
    qn = jnp.dot(q, n_scr[...].astype(BF16), preferred_element_type=F32)[:, 0:1]
    den = jnp.sum(s, axis=-1, keepdims=True) + w_inter * qn
    h_ref[...] = num / jnp.maximum(jnp.abs(den), jnp.exp(-m_row))

    bl = jnp.sum(lf_row, axis=-1, keepdims=True)
    a_row = bl - b_row + i_row
    m_new = jnp.maximum(bl + m_prev, jnp.max(a_row, axis=-1, keepdims=True))
    decay = jnp.exp(bl + m_prev - m_new)
    wk_t = k.astype(F32).T * jnp.exp(a_row - m_new)
    c_scr[...] = decay * ct + jnp.dot(wk_t.astype(BF16), v, preferred_element_type=F32)
    n_scr[...] = decay * n_scr[...] + jnp.sum(wk_t, axis=-1, keepdims=True)
    m_scr[...] = jnp.broadcast_to(m_new, m_scr.shape)


def _mlstm_scan(qkv, gates, seqs, cfg, *, reverse):
    t = qkv.shape[0]
    nh, dqk, dv, chunk = cfg.ml_heads, cfg.ml_dqk, cfg.ml_dv, cfg.ml_chunk
    chunk = min(chunk, min(n for _, n in seqs))
    assert all(s % chunk == 0 and n % chunk == 0 for s, n in seqs)
    nc = t // chunk
    if reverse:
        resets = tuple((s + n) // chunk - 1 for s, n in seqs)
    else:
        resets = tuple(s // chunk for s, _ in seqs)

    def ci(c):
        return nc - 1 - c if reverse else c

    v_off = 2 * nh * dqk // dv
    assert (2 * nh * dqk) % dv == 0
    return pl.pallas_call(
        functools.partial(_mlstm_body, chunk=chunk, nc=nc, reverse=reverse, reset_chunks=resets,
                          m_init=cfg.ml_m_init),
        grid=(nh, nc),
        in_specs=[pl.BlockSpec((chunk, dqk), lambda h, c: (ci(c), h)),
                  pl.BlockSpec((chunk, dqk), lambda h, c: (ci(c), nh + h)),
                  pl.BlockSpec((chunk, dv), lambda h, c: (ci(c), v_off + h)),
                  pl.BlockSpec((1, 8, chunk), lambda h, c: (h, 0, ci(c)))],
        out_specs=pl.BlockSpec((chunk, dv), lambda h, c: (ci(c), h)),
        out_shape=jax.ShapeDtypeStruct((t, nh * dv), F32),
        scratch_shapes=[pltpu.VMEM((dqk, dv), F32), pltpu.VMEM((dqk, LANE), F32), pltpu.VMEM((8, LANE), F32)],
        compiler_params=_params(("parallel", "arbitrary")),
        name="mlstm_scan_bw" if reverse else "mlstm_scan_fw",
    )(qkv, qkv, qkv, gates)


def _mlstm_combine_body(hf_ref, hb_ref, o_ref, g_ref, out_ref, *, nh, dv, eps):
    for hh in range(nh):
        sl = slice(hh * dv, (hh + 1) * dv)
        h = hf_ref[:, sl] + hb_ref[:, sl]
        h = h * lax.rsqrt(jnp.mean(h * h, axis=-1, keepdims=True) + eps)
        out_ref[:, sl] = (h * g_ref[:, sl] * o_ref[:, sl].astype(F32)).astype(out_ref.dtype)


def _mlstm_combine(h_fw, h_bw, proj, norm_g, cfg):
    t, w = h_fw.shape
    tr = min(cfg.t_row, t)
    o_off = (2 * cfg.ml_heads * cfg.ml_dqk + w) // w
    assert (2 * cfg.ml_heads * cfg.ml_dqk) % w == 0
    row = pl.BlockSpec((tr, w), lambda i: (i, 0))
    return pl.pallas_call(
        functools.partial(_mlstm_combine_body, nh=cfg.ml_heads, dv=cfg.ml_dv, eps=cfg.rms_eps),
        grid=(t // tr,),
        in_specs=[row, row, pl.BlockSpec((tr, w), lambda i: (i, o_off)), pl.BlockSpec((1, w), lambda i: (0, 0))],
        out_specs=row,
        out_shape=jax.ShapeDtypeStruct((t, w), BF16),
        compiler_params=_params(("parallel",)),
        name="mlstm_combine",
    )(h_fw, h_bw, proj, norm_g.reshape(1, w))


def _mlstm_mixer(xf, xb, w_in, b_gates, norm_g, w_out, seqs, cfg):
    t = xf.shape[0]
    nh, dqk, dv = cfg.ml_heads, cfg.ml_dqk, cfg.ml_dv
    qk_w, v_w = nh * dqk, nh * dv
    main_w = 2 * qk_w + 2 * v_w
    n_gates = 4 * nh
    col_scale = jnp.concatenate([jnp.ones((qk_w,), F32), jnp.full((qk_w,), dqk ** -0.5, F32),
                                 jnp.ones((2 * v_w,), F32)])
    w_main = (w_in[:, :main_w] * col_scale).astype(BF16)
    w_gate = jnp.pad(w_in[:, main_w:], ((0, 0), (0, LANE - n_gates))).astype(BF16)
    b_gate = jnp.pad(b_gates, (0, LANE - n_gates)).reshape(1, LANE)

    tm, tn, tk = cfg.mm_wide
    tm, tn = min(tm, t), min(tn, v_w)
    assert (2 * qk_w + v_w) % tn == 0
    proj = _matmul(xb, w_main, tm=tm, tn=tn, tk=tk,
                   epilogue=functools.partial(_ep_sigmoid_from, first_gate_tile=(2 * qk_w + v_w) // tn),
                   out_shape=jax.ShapeDtypeStruct((t, main_w), BF16),
                   out_specs=pl.BlockSpec((tm, tn), lambda i, j, k: (i, j)), name="mlstm_in_proj")
    gates = _matmul(xb, w_gate, tm=tm, tn=LANE, tk=tk, epilogue=_ep_bias,
                    out_shape=jax.ShapeDtypeStruct((t, LANE), F32),
                    out_specs=pl.BlockSpec((tm, LANE), lambda i, j, k: (i, 0)),
                    extra=(b_gate,), extra_specs=(pl.BlockSpec((1, LANE), lambda i, j, k: (0, 0)),),
                    name="mlstm_gate_proj")
    g = gates[:, :n_gates].reshape(t, 4, nh).transpose(2, 1, 0)
    g = jnp.pad(g, ((0, 0), (0, 4), (0, 0)))
    h_fw = _mlstm_scan(proj, g, seqs, cfg, reverse=False)
    h_bw = _mlstm_scan(proj, g, seqs, cfg, reverse=True)
    hact = _mlstm_combine(h_fw, h_bw, proj, norm_g, cfg)
    return _mm_residual(hact, w_out.astype(BF16), xf, cfg.mm_wide, cfg, "mlstm_out_proj")


def _rope_tables(max_len, cfg, scale):
    half = cfg.rope // 2
    inv_freq = jnp.power(cfg.rope_base, -jnp.arange(0, cfg.rope, 2, dtype=F32) / cfg.rope)
    ang = jnp.arange(max_len, dtype=F32)[:, None] * inv_freq[None, :]
    cos, sin = jnp.cos(ang), jnp.sin(ang)
    z = jnp.zeros((max_len, LANE - 2 * half), F32)
    zh = jnp.zeros((max_len, half), F32)
    cos_t = jnp.concatenate([cos, cos, z], axis=1) * scale
    sin_a = jnp.concatenate([-sin, zh, z], axis=1) * scale
    sin_b = jnp.concatenate([zh, sin, z], axis=1) * scale
    return cos_t, sin_a, sin_b


def _rotate(x, cos_t, sin_a, sin_b, half):
    return x * cos_t + pltpu.roll(x, LANE - half, 1) * sin_a + pltpu.roll(x, half, 1) * sin_b


def _ep_mla_in(acc, extra, outs, j, *, q_rank, kv_rank, eps):
    gq_ref, gkv_ref = extra
    cq_ref, ckv_ref, kr_ref = outs

    def rms(x, g):
        return x * lax.rsqrt(jnp.mean(x * x, axis=-1, keepdims=True) + eps) * g

    cq_ref[...] = rms(acc[:, :q_rank], gq_ref[...]).astype(cq_ref.dtype)
    ckv_ref[...] = rms(acc[:, q_rank:q_rank + kv_rank], gkv_ref[...]).astype(ckv_ref.dtype)
    kr_ref[...] = acc[:, q_rank + kv_rank:]


def _ep_q_up(acc, extra, outs, j, *, heads, scale, half):
    cos_ref, sa_ref, sb_ref = extra
    out = outs[0]
    for hh in range(heads):
        lo = hh * 2 * LANE
        out[:, lo:lo + LANE] = (acc[:, lo:lo + LANE] * scale).astype(out.dtype)
        r = acc[:, lo + LANE:lo + 2 * LANE]
        out[:, lo + LANE:lo + 2 * LANE] = _rotate(r, cos_ref[...], sa_ref[...], sb_ref[...], half).astype(out.dtype)


def _kv_up_body(x_ref, wk_ref, wv_ref, kr_ref, cos_ref, sa_ref, sb_ref, k_out, v_out, *, heads, half):
    x = x_ref[...]
    kk = jnp.dot(x, wk_ref[...], preferred_element_type=F32)
    vv = jnp.dot(x, wv_ref[...], preferred_element_type=F32)
    kr = _rotate(kr_ref[...], cos_ref[...], sa_ref[...], sb_ref[...], half).astype(k_out.dtype)
    ones = jnp.ones((x.shape[0], LANE), v_out.dtype)
    for hh in range(heads):
        lo = hh * 2 * LANE
        k_out[:, lo:lo + LANE] = kk[:, hh * LANE:(hh + 1) * LANE].astype(k_out.dtype)
        k_out[:, lo + LANE:lo + 2 * LANE] = kr
        v_out[:, lo:lo + LANE] = vv[:, hh * LANE:(hh + 1) * LANE].astype(v_out.dtype)
        v_out[:, lo + LANE:lo + 2 * LANE] = ones


def _pos_block(seqs, tm):
    def f(i):
        out = i
        for s, _ in seqs:
            out = jnp.where(i >= s // tm, i - s // tm, out)
        return out
    return f


def _attn_body(q_ref, k_ref, v_ref, o_ref, m_scr, acc_scr, *, nkv, heads, qw, vd):
    ki = pl.program_id(3)

    @pl.when(ki == 0)
    def _():
        m_scr[...] = jnp.full_like(m_scr, -jnp.inf)
        acc_scr[...] = jnp.zeros_like(acc_scr)

    tkv = k_ref.shape[0]
    for hh in range(heads):
        q = q_ref[:, hh * qw:(hh + 1) * qw]
        k = k_ref[:, hh * qw:(hh + 1) * qw]
        v = v_ref[:, hh * 2 * vd:(hh + 1) * 2 * vd]
        s = lax.dot_general(q, k, (((1,), (1,)), ((), ())), preferred_element_type=F32)
        m_prev = m_scr[hh]
        m_new = jnp.maximum(m_prev, jnp.broadcast_to(jnp.max(s, axis=-1, keepdims=True), m_prev.shape))
        alpha = jnp.exp2(m_prev - m_new)
        p = jnp.exp2(s - pltpu.repeat(m_new, tkv // LANE, 1))
        pv = jnp.dot(p.astype(v.dtype), v, preferred_element_type=F32)
        acc_scr[hh] = pltpu.repeat(alpha, 2 * vd // LANE, 1) * acc_scr[hh] + pv
        m_scr[hh] = m_new

    @pl.when(ki == nkv - 1)
    def _():
        for hh in range(heads):
            o_ref[:, hh * vd:(hh + 1) * vd] = (acc_scr[hh, :, :vd] / acc_scr[hh, :, vd:]).astype(o_ref.dtype)


def _attention(q, k, v, row0, nseq, slen, cfg):
    nh, vd, qw = cfg.mla_heads, cfg.vdim, cfg.qk_pad
    assert vd == LANE
    hp = min(cfg.attn_heads_per_step, nh)
    tq, tkv = min(cfg.tq, slen), min(cfg.tkv, slen)
    assert slen % tq == 0 and slen % tkv == 0 and row0 % tq == 0 and row0 % tkv == 0 and nh % hp == 0
    nq, nkv = slen // tq, slen // tkv
    q0, k0 = row0 // tq, row0 // tkv
    return pl.pallas_call(
        functools.partial(_attn_body, nkv=nkv, heads=hp, qw=qw, vd=vd),
        grid=(nseq, nh // hp, nq, nkv),
        in_specs=[pl.BlockSpec((tq, hp * qw), lambda b, h, qi, ki: (q0 + b * nq + qi, h)),
                  pl.BlockSpec((tkv, hp * qw), lambda b, h, qi, ki: (k0 + b * nkv + ki, h)),
                  pl.BlockSpec((tkv, hp * 2 * vd), lambda b, h, qi, ki: (k0 + b * nkv + ki, h))],
        out_specs=pl.BlockSpec((tq, hp * vd), lambda b, h, qi, ki: (b * nq + qi, h)),
        out_shape=jax.ShapeDtypeStruct((nseq * slen, nh * vd), BF16),
        scratch_shapes=[pltpu.VMEM((hp, tq, LANE), F32), pltpu.VMEM((hp, tq, 2 * vd), F32)],
        compiler_params=_params(("parallel", "parallel", "parallel", "arbitrary")),
        name="mla_attention",
    )(q, k, v)


def _mla_mixer(xf, xb, w_in, q_norm_g, kv_norm_g, w_uq, w_ukv, w_out, seqs, cfg):
    t, d = xf.shape
    nh, nope, rope, vd = cfg.mla_heads, cfg.nope, cfg.rope, cfg.vdim
    qr, kvr = cfg.q_rank, cfg.kv_rank
    half = rope // 2
    assert nope == LANE and vd == LANE and rope <= LANE
    qw = cfg.qk_pad
    tm = min(cfg.mm_mla_in[0], t)
    max_len = max(n for _, n in seqs)
    pos = _pos_block(seqs, tm)

    w_in_p = jnp.pad(w_in, ((0, 0), (0, LANE - rope))).astype(BF16)
    w_uq_p = jnp.pad(w_uq.reshape(qr, nh, nope + rope), ((0, 0), (0, 0), (0, qw - nope - rope)))
    w_uq_p = w_uq_p.reshape(qr, nh * qw).astype(BF16)
    w_ukv_h = w_ukv.reshape(kvr, nh, nope + vd)
    w_uk = w_ukv_h[:, :, :nope].reshape(kvr, nh * nope).astype(BF16)
    w_uv = w_ukv_h[:, :, nope:].reshape(kvr, nh * vd).astype(BF16)

    n_in = qr + kvr + LANE
    cq, ckv, kr = _matmul(
        xb, w_in_p, tm=tm, tn=n_in, tk=cfg.mm_mla_in[2],
        epilogue=functools.partial(_ep_mla_in, q_rank=qr, kv_rank=kvr, eps=cfg.rms_eps),
        out_shape=[jax.ShapeDtypeStruct((t, qr), BF16), jax.ShapeDtypeStruct((t, kvr), BF16),
                   jax.ShapeDtypeStruct((t, LANE), F32)],
        out_specs=[pl.BlockSpec((tm, qr), lambda i, j, k: (i, 0)),
                   pl.BlockSpec((tm, kvr), lambda i, j, k: (i, 0)),
                   pl.BlockSpec((tm, LANE), lambda i, j, k: (i, 0))],
        extra=(q_norm_g.reshape(1, qr), kv_norm_g.reshape(1, kvr)),
        extra_specs=(pl.BlockSpec((1, qr), lambda i, j, k: (0, 0)),
                     pl.BlockSpec((1, kvr), lambda i, j, k: (0, 0))),
        name="mla_in_proj")

    scale = (nope + rope) ** -0.5 * LOG2_E
    q_tabs = _rope_tables(max_len, cfg, scale)
    k_tabs = _rope_tables(max_len, cfg, 1.0)
    tn = min(cfg.mm_q_up[1], nh * qw)
    tab3 = pl.BlockSpec((tm, LANE), lambda i, j, k: (pos(i), 0))
    q = _matmul(cq, w_uq_p, tm=tm, tn=tn, tk=cfg.mm_q_up[2],
                epilogue=functools.partial(_ep_q_up, heads=tn // qw, scale=scale, half=half),
                out_shape=jax.ShapeDtypeStruct((t, nh * qw), BF16),
                out_specs=pl.BlockSpec((tm, tn), lambda i, j, k: (i, j)),
                extra=q_tabs, extra_specs=(tab3, tab3, tab3), name="mla_q_up")

    hb = min(cfg.kv_heads_per_step, nh)
    tab2 = pl.BlockSpec((tm, LANE), lambda i, j: (pos(i), 0))
    k, v = pl.pallas_call(
        functools.partial(_kv_up_body, heads=hb, half=half),
        grid=(t // tm, nh // hb),
        in_specs=[pl.BlockSpec((tm, kvr), lambda i, j: (i, 0)),
                  pl.BlockSpec((kvr, hb * nope), lambda i, j: (0, j)),
                  pl.BlockSpec((kvr, hb * vd), lambda i, j: (0, j)),
                  pl.BlockSpec((tm, LANE), lambda i, j: (i, 0)),
                  tab2, tab2, tab2],
        out_specs=[pl.BlockSpec((tm, hb * qw), lambda i, j: (i, j)),
                   pl.BlockSpec((tm, hb * 2 * vd), lambda i, j: (i, j))],
        out_shape=[jax.ShapeDtypeStruct((t, nh * qw), BF16), jax.ShapeDtypeStruct((t, nh * 2 * vd), BF16)],
        compiler_params=_params(("parallel", "parallel")),
        name="mla_kv_up",
    )(ckv, w_uk, w_uv, kr, *k_tabs)

    outs = []
    idx = 0
    while idx < len(seqs):
        row0, slen = seqs[idx]
        nseq = 1
        while idx + nseq < len(seqs) and seqs[idx + nseq] == (row0 + nseq * slen, slen):
            nseq += 1
        outs.append(_attention(q, k, v, row0, nseq, slen, cfg))
        idx += nseq
    o = outs[0] if len(outs) == 1 else jnp.concatenate(outs, axis=0)
    return _mm_residual(o, w_out.astype(BF16), xf, cfg.mm_wide, cfg, "mla_out_proj")


def _conv_ffn(xf, xb, w_up, conv_w, conv_b, w_down, seqs, cfg):
    f, fp = cfg.d_ff, cfg.d_ff_pad
    pad = fp - f

    def pad_halves(a):
        lead = [(0, 0)] * (a.ndim - 1)
        return jnp.concatenate([jnp.pad(a[..., :f], lead + [(0, pad)]), jnp.pad(a[..., f:], lead + [(0, pad)])], -1)

    w_up_p = pad_halves(w_up).astype(BF16)
    w_down_p = jnp.pad(w_down, ((0, pad), (0, 0))).astype(BF16)
    h = _mm_plain(xb, w_up_p, cfg.mm_wide, BF16, "ffn_up")
    act = _conv_gate(h, pad_halves(conv_w), pad_halves(conv_b.reshape(1, -1)), seqs, cfg)
    return _mm_residual(act, w_down_p, xf, cfg.mm_down, cfg, "ffn_down")


def _trunk(x, seqs, cfg, ml_w_in, ml_b_gates, ml_norm_g, ml_w_out, mla_w_in, mla_q_norm_g, mla_kv_norm_g,
           mla_w_uq, mla_w_ukv, mla_w_out, ffn_w_up, ffn_conv_w, ffn_conv_b, ffn_w_down,
           ln1_g, ln1_b, ln2_g, ln2_b):
    xf, xb = x, x.astype(BF16)
    for i in range(cfg.depth):
        j = i // 2
        if i % 2 == 0:
            y = _mlstm_mixer(xf, xb, ml_w_in[j], ml_b_gates[j], ml_norm_g[j], ml_w_out[j], seqs, cfg)
        else:
            y = _mla_mixer(xf, xb, mla_w_in[j], mla_q_norm_g[j], mla_kv_norm_g[j],
                           mla_w_uq[j], mla_w_ukv[j], mla_w_out[j], seqs, cfg)
        xf, xb = _layer_norm(y, ln1_g[i], ln1_b[i], cfg)
        y = _conv_ffn(xf, xb, ffn_w_up[i], ffn_conv_w[i], ffn_conv_b[i], ffn_w_down[i], seqs, cfg)
        xf, xb = _layer_norm(y, ln2_g[i], ln2_b[i], cfg)
    return xf


def _run(cfg, x_prompt, x_sample, *weights):
    d = x_prompt.shape[-1]
    shapes = [x_prompt.shape, x_sample.shape]
    seqs, row = [], 0
    for b, s, _ in shapes:
        for _ in range(b):
            seqs.append((row, s))
            row += s
    x = jnp.concatenate([x_prompt.reshape(-1, d), x_sample.reshape(-1, d)], axis=0)
    y = _trunk(x, tuple(seqs), cfg, *weights)
    n_prompt = shapes[0][0] * shapes[0][1]
    return y[:n_prompt].reshape(shapes[0]), y[n_prompt:].reshape(shapes[1])


def kernel(x_prompt, x_sample, ml_w_in, ml_b_gates, ml_norm_g, ml_w_out, mla_w_in, mla_q_norm_g, mla_kv_norm_g, mla_w_uq, mla_w_ukv, mla_w_out, ffn_w_up, ffn_conv_w, ffn_conv_b, ffn_w_down, ln1_g, ln1_b, ln2_g, ln2_b):
    return _run(Cfg(), x_prompt, x_sample, ml_w_in, ml_b_gates, ml_norm_g, ml_w_out, mla_w_in, mla_q_norm_g,
                mla_kv_norm_g, mla_w_uq, mla_w_ukv, mla_w_out, ffn_w_up, ffn_conv_w, ffn_conv_b, ffn_w_down,
                ln1_g, ln1_b, ln2_g, ln2_b)
```

```python
import functools
from typing import NamedTuple

import jax
import jax.numpy as jnp
import numpy as np
from jax import lax
from jax.experimental import pallas as pl
from jax.experimental.pallas import tpu as pltpu

F32 = jnp.float32
BF16 = jnp.bfloat16

LANE = 128
MXU_DIM = 256
LOG2_E = 1.4426950408889634
VMEM_LIMIT = 56 * 1024 * 1024


class Cfg(NamedTuple):
    d_model: int = 4096
    depth: int = 4
    ml_heads: int = 8
    ml_dqk: int = 256
    ml_dv: int = 512
    ml_chunk: int = 512
    ml_m_init: float = -1e30
    mla_heads: int = 32
    nope: int = 128
    rope: int = 64
    vdim: int = 128
    q_rank: int = 1024
    kv_rank: int = 512
    rope_base: float = 10000.0
    d_ff: int = 11008
    d_ff_pad: int = 11264
    ln_eps: float = 1e-5
    rms_eps: float = 1e-6
    mm_wide: tuple = (1024, 1024, 4096)
    mm_down: tuple = (1024, 1024, 2816)
    mm_mla_in: tuple = (1024, 0, 512)
    mm_q_up: tuple = (1024, 1024, 1024)
    mm_halo_tn: int = 1024
    ff_block: int = 512
    t_row: int = 256
    tq: int = 1024
    tkv: int = 1024
    attn_heads_per_step: int = 8
    kv_heads_per_step: int = 4

    @property
    def dn_alpha(self):
        return (2 * self.depth) ** 0.25

    @property
    def qk_pad(self):
        return 2 * LANE


def _params(sem, vmem=VMEM_LIMIT):
    return pltpu.CompilerParams(dimension_semantics=sem, vmem_limit_bytes=vmem)


def _mm_body(*refs, nk, n_extra, epilogue):
    x_ref, w_ref = refs[0], refs[1]
    extra = refs[2:2 + n_extra]
    ij, k = (pl.program_id(0), pl.program_id(1)), pl.program_id(2)
    part = jnp.dot(x_ref[...], w_ref[...], preferred_element_type=F32)
    if nk == 1:
        epilogue(part, extra, refs[2 + n_extra:], ij)
        return
    outs, acc_ref = refs[2 + n_extra:-1], refs[-1]

    @pl.when(k == 0)
    def _():
        acc_ref[...] = part

    @pl.when(jnp.logical_and(k > 0, k < nk - 1))
    def _():
        acc_ref[...] += part

    @pl.when(k == nk - 1)
    def _():
        epilogue(acc_ref[...] + part, extra, outs, ij)


def _matmul(x, w, *, tm, tn, tk, epilogue, out_shape, out_specs, extra=(), extra_specs=(), name):
    m, kdim = x.shape
    n = w.shape[1]
    tm, tn, tk = min(tm, m), min(tn, n), min(tk, kdim)
    assert m % tm == 0 and n % tn == 0 and kdim % tk == 0, (x.shape, w.shape, tm, tn, tk)
    nk = kdim // tk
    return pl.pallas_call(
        functools.partial(_mm_body, nk=nk, n_extra=len(extra), epilogue=epilogue),
        grid=(m // tm, n // tn, nk),
        in_specs=[pl.BlockSpec((tm, tk), lambda i, j, k: (i, k)),
                  pl.BlockSpec((tk, tn), lambda i, j, k: (k, j)),
                  *extra_specs],
        out_specs=out_specs,
        out_shape=out_shape,
        scratch_shapes=[] if nk == 1 else [pltpu.VMEM((tm, tn), F32)],
        compiler_params=_params(("parallel", "parallel", "arbitrary")),
        name=name,
    )(x, w, *extra)


def _ep_store(acc, extra, outs, ij):
    outs[0][...] = acc.astype(outs[0].dtype)


def _ep_residual(acc, extra, outs, ij, *, alpha):
    outs[0][...] = alpha * extra[0][...] + acc


def _ep_bias(acc, extra, outs, ij):
    outs[0][...] = acc + extra[0][...]


def _ep_sigmoid_from(acc, extra, outs, ij, *, first_gate_tile):
    @pl.when(ij[1] < first_gate_tile)
    def _():
        outs[0][...] = acc.astype(outs[0].dtype)

    @pl.when(ij[1] >= first_gate_tile)
    def _():
        outs[0][...] = jax.nn.sigmoid(acc).astype(outs[0].dtype)


def _mm_plain(x, w, tiles, out_dtype, name):
    m, n = x.shape[0], w.shape[1]
    tm, tn, tk = tiles
    tm, tn = min(tm, m), min(tn, n)
    return _matmul(x, w, tm=tm, tn=tn, tk=tk, epilogue=_ep_store,
                   out_shape=jax.ShapeDtypeStruct((m, n), out_dtype),
                   out_specs=pl.BlockSpec((tm, tn), lambda i, j, k: (i, j)), name=name)


def _mm_residual(x, w, resid, tiles, cfg, name):
    m, n = x.shape[0], w.shape[1]
    tm, tn, tk = tiles
    tm, tn = min(tm, m), min(tn, n)
    spec = pl.BlockSpec((tm, tn), lambda i, j, k: (i, j))
    return _matmul(x, w, tm=tm, tn=tn, tk=tk,
                   epilogue=functools.partial(_ep_residual, alpha=cfg.dn_alpha),
                   out_shape=jax.ShapeDtypeStruct((m, n), F32), out_specs=spec,
                   extra=(resid,), extra_specs=(spec,), name=name)


def _ln_body(y_ref, g_ref, b_ref, xf_ref, xb_ref, *, eps):
    y = y_ref[...]
    mu = jnp.mean(y, axis=-1, keepdims=True)
    d = y - mu
    var = jnp.mean(d * d, axis=-1, keepdims=True)
    out = d * lax.rsqrt(var + eps) * g_ref[...] + b_ref[...]
    xf_ref[...] = out
    xb_ref[...] = out.astype(BF16)


def _layer_norm(y, g, b, cfg):
    t, d = y.shape
    tr = min(cfg.t_row, t)
    row = pl.BlockSpec((tr, d), lambda i: (i, 0))
    vec = pl.BlockSpec((1, d), lambda i: (0, 0))
    return pl.pallas_call(
        functools.partial(_ln_body, eps=cfg.ln_eps),
        grid=(t // tr,),
        in_specs=[row, vec, vec],
        out_specs=[row, row],
        out_shape=[jax.ShapeDtypeStruct((t, d), F32), jax.ShapeDtypeStruct((t, d), BF16)],
        compiler_params=_params(("parallel",)),
        name="layer_norm",
    )(y, g.reshape(1, d), b.reshape(1, d))


def _mlstm_body(q_ref, k_ref, v_ref, g_ref, *rest, chunk, nc, reverse, reset_chunks, m_init, combine_eps):
    if combine_eps is None:
        h_ref, c_scr, n_scr, m_scr = rest
    else:
        hf_ref, og_ref, ng_ref, h_ref, c_scr, n_scr, m_scr = rest
    c = pl.program_id(1)
    cc = nc - 1 - c if reverse else c

    @pl.when(functools.reduce(jnp.logical_or, [cc == r for r in reset_chunks]))
    def _():
        c_scr[...] = jnp.zeros_like(c_scr)
        n_scr[...] = jnp.zeros_like(n_scr)
        m_scr[...] = jnp.full_like(m_scr, m_init)

    gates = g_ref[0]
    ri, rf = (2, 3) if reverse else (0, 1)
    i_row = gates[ri:ri + 1, :]
    lf_row = jax.nn.log_sigmoid(gates[rf:rf + 1, :])

    row = lax.broadcasted_iota(jnp.int32, (chunk, chunk), 0)
    col = lax.broadcasted_iota(jnp.int32, (chunk, chunk), 1)
    visible = (col >= row) if reverse else (col <= row)
    visible_t = (row >= col) if reverse else (row <= col)
    b_col = jnp.sum(jnp.where(visible, lf_row, 0.0), axis=-1, keepdims=True)
    lf_col = jnp.sum(jnp.where(row == col, lf_row, 0.0), axis=-1, keepdims=True)
    b_row = jnp.sum(jnp.where(visible_t, lf_col, 0.0), axis=0, keepdims=True)

    m_prev = m_scr[0:1, 0:1]
    logd = jnp.where(visible, b_col - b_row + i_row, -jnp.inf)
    inter = b_col + m_prev
    m_row = jnp.maximum(jnp.max(logd, axis=-1, keepdims=True), inter)
    dmat = jnp.exp(logd - m_row)

    q = q_ref[...]
    k = k_ref[...]
    v = v_ref[...]
    s = lax.dot_general(q, k, (((1,), (1,)), ((), ())), preferred_element_type=F32) * dmat
    w_inter = jnp.exp(inter - m_row)
    ct = c_scr[...]
    qc = jnp.dot(q, ct.astype(BF16), preferred_element_type=F32)
    num = jnp.dot(s.astype(BF16), v, preferred_element_type=F32) + w_inter * qc
    qn = jnp.dot(q, n_scr[...].astype(BF16), preferred_element_type=F32)[:, 0:1]
    den = jnp.sum(s, axis=-1, keepdims=True) + w_inter * qn
    h = num / jnp.maximum(jnp.abs(den), jnp.exp(-m_row))
    if combine_eps is None:
        h_ref[...] = h
    else:
        h = h + hf_ref[...]
        h = h * lax.rsqrt(jnp.mean(h * h, axis=-1, keepdims=True) + combine_eps)
        h_ref[...] = (h * ng_ref[...] * og_ref[...].astype(F32)).astype(h_ref.dtype)

    bl = jnp.sum(lf_row, axis=-1, keepdims=True)
    a_row = bl - b_row + i_row
    m_new = jnp.maximum(bl + m_prev, jnp.max(a_row, axis=-1, keepdims=True))
    decay = jnp.exp(bl + m_prev - m_new)
    wk_t = k.astype(F32).T * jnp.exp(a_row - m_new)
    c_scr[...] = decay * ct + jnp.dot(wk_t.astype(BF16), v, preferred_element_type=F32)
    n_scr[...] = decay * n_scr[...] + jnp.sum(wk_t, axis=-1, keepdims=True)
    m_scr[...] = jnp.broadcast_to(m_new, m_scr.shape)


def _mlstm_scan(qkv, gates, seqs, cfg, *, reverse, h_other=None, norm_g=None):
    t = qkv.shape[0]
    nh, dqk, dv, chunk = cfg.ml_heads, cfg.ml_dqk, cfg.ml_dv, cfg.ml_chunk
    chunk = min(chunk, min(n for _, n in seqs))
    assert all(s % chunk == 0 and n % chunk == 0 for s, n in seqs)
    nc = t // chunk
    if reverse:
        resets = tuple((s + n) // chunk - 1 for s, n in seqs)
    else:
        resets = tuple(s // chunk for s, _ in seqs)

    def ci(c):
        return nc - 1 - c if reverse else c

    assert (2 * nh * dqk) % dv == 0
    v_off = 2 * nh * dqk // dv
    o_off = v_off + nh
    combine = h_other is not None
    h_spec = pl.BlockSpec((chunk, dv), lambda h, c: (ci(c), h))
    in_specs = [pl.BlockSpec((chunk, dqk), lambda h, c: (ci(c), h)),
                pl.BlockSpec((chunk, dqk), lambda h, c: (ci(c), nh + h)),
                pl.BlockSpec((chunk, dv), lambda h, c: (ci(c), v_off + h)),
                pl.BlockSpec((1, 8, chunk), lambda h, c: (h, 0, ci(c)))]
    operands = [qkv, qkv, qkv, gates]
    if combine:
        in_specs += [h_spec, pl.BlockSpec((chunk, dv), lambda h, c: (ci(c), o_off + h)),
                     pl.BlockSpec((1, dv), lambda h, c: (0, h))]
        operands += [h_other, qkv, norm_g.reshape(1, nh * dv)]
    return pl.pallas_call(
        functools.partial(_mlstm_body, chunk=chunk, nc=nc, reverse=reverse, reset_chunks=resets,
                          m_init=cfg.ml_m_init, combine_eps=cfg.rms_eps if combine else None),
        grid=(nh, nc),
        in_specs=in_specs,
        out_specs=h_spec,
        out_shape=jax.ShapeDtypeStruct((t, nh * dv), BF16 if combine else F32),
        scratch_shapes=[pltpu.VMEM((dqk, dv), F32), pltpu.VMEM((dqk, LANE), F32), pltpu.VMEM((8, LANE), F32)],
        compiler_params=_params(("parallel", "arbitrary")),
        name="mlstm_scan_bw" if reverse else "mlstm_scan_fw",
    )(*operands)


def _mlstm_mixer(xf, xb, w_in, b_gates, norm_g, w_out, seqs, cfg):
    t = xf.shape[0]
    nh, dqk, dv = cfg.ml_heads, cfg.ml_dqk, cfg.ml_dv
    qk_w, v_w = nh * dqk, nh * dv
    main_w = 2 * qk_w + 2 * v_w
    n_gates = 4 * nh
    col_scale = jnp.concatenate([jnp.ones((qk_w,), F32), jnp.full((qk_w,), dqk ** -0.5, F32),
                                 jnp.ones((2 * v_w,), F32)])
    w_main = (w_in[:, :main_w] * col_scale).astype(BF16)
    w_gate = jnp.pad(w_in[:, main_w:], ((0, 0), (0, LANE - n_gates))).astype(BF16)
    b_gate = jnp.pad(b_gates, (0, LANE - n_gates)).reshape(1, LANE)

    tm, tn, tk = cfg.mm_wide
    tm, tn = min(tm, t), min(tn, v_w)
    assert (2 * qk_w + v_w) % tn == 0
    proj = _matmul(xb, w_main, tm=tm, tn=tn, tk=tk,
                   epilogue=functools.partial(_ep_sigmoid_from, first_gate_tile=(2 * qk_w + v_w) // tn),
                   out_shape=jax.ShapeDtypeStruct((t, main_w), BF16),
                   out_specs=pl.BlockSpec((tm, tn), lambda i, j, k: (i, j)), name="mlstm_in_proj")
    gates = _matmul(xb, w_gate, tm=tm, tn=LANE, tk=tk, epilogue=_ep_bias,
                    out_shape=jax.ShapeDtypeStruct((t, LANE), F32),
                    out_specs=pl.BlockSpec((tm, LANE), lambda i, j, k: (i, 0)),
                    extra=(b_gate,), extra_specs=(pl.BlockSpec((1, LANE), lambda i, j, k: (0, 0)),),
                    name="mlstm_gate_proj")
    g = gates[:, :n_gates].reshape(t, 4, nh).transpose(2, 1, 0)
    g = jnp.pad(g, ((0, 0), (0, 4), (0, 0)))
    h_fw = _mlstm_scan(proj, g, seqs, cfg, reverse=False)
    hact = _mlstm_scan(proj, g, seqs, cfg, reverse=True, h_other=h_fw, norm_g=norm_g)
    return _mm_residual(hact, w_out.astype(BF16), xf, cfg.mm_wide, cfg, "mlstm_out_proj")


def _rope_tables(max_len, cfg, scale):
    half = cfg.rope // 2
    inv_freq = jnp.power(cfg.rope_base, -jnp.arange(0, cfg.rope, 2, dtype=F32) / cfg.rope)
    ang = jnp.arange(max_len, dtype=F32)[:, None] * inv_freq[None, :]
    cos, sin = jnp.cos(ang), jnp.sin(ang)
    z = jnp.zeros((max_len, LANE - 2 * half), F32)
    zh = jnp.zeros((max_len, half), F32)
    cos_t = jnp.concatenate([cos, cos, z], axis=1) * scale
    sin_a = jnp.concatenate([-sin, zh, z], axis=1) * scale
    sin_b = jnp.concatenate([zh, sin, z], axis=1) * scale
    return cos_t, sin_a, sin_b


def _rotate(x, cos_t, sin_a, sin_b, half):
    return x * cos_t + pltpu.roll(x, LANE - half, 1) * sin_a + pltpu.roll(x, half, 1) * sin_b


def _ep_mla_in(acc, extra, outs, ij, *, q_rank, kv_rank, eps):
    gq_ref, gkv_ref = extra
    cq_ref, ckv_ref, kr_ref = outs

    def rms(x, g):
        return x * lax.rsqrt(jnp.mean(x * x, axis=-1, keepdims=True) + eps) * g

    cq_ref[...] = rms(acc[:, :q_rank], gq_ref[...]).astype(cq_ref.dtype)
    ckv_ref[...] = rms(acc[:, q_rank:q_rank + kv_rank], gkv_ref[...]).astype(ckv_ref.dtype)
    kr_ref[...] = acc[:, q_rank + kv_rank:]


def _ep_q_up(acc, extra, outs, ij, *, heads, scale, half):
    cos_ref, sa_ref, sb_ref = extra
    out = outs[0]
    for hh in range(heads):
        lo = hh * 2 * LANE
        out[:, lo:lo + LANE] = (acc[:, lo:lo + LANE] * scale).astype(out.dtype)
        r = acc[:, lo + LANE:lo + 2 * LANE]
        out[:, lo + LANE:lo + 2 * LANE] = _rotate(r, cos_ref[...], sa_ref[...], sb_ref[...], half).astype(out.dtype)


def _kv_up_body(x_ref, wk_ref, wv_ref, kr_ref, cos_ref, sa_ref, sb_ref, k_out, v_out, *, heads, half):
    x = x_ref[...]
    kk = jnp.dot(x, wk_ref[...], preferred_element_type=F32)
    vv = jnp.dot(x, wv_ref[...], preferred_element_type=F32)
    kr = _rotate(kr_ref[...], cos_ref[...], sa_ref[...], sb_ref[...], half).astype(k_out.dtype)
    ones = jnp.ones((x.shape[0], LANE), v_out.dtype)
    for hh in range(heads):
        lo = hh * 2 * LANE
        k_out[:, lo:lo + LANE] = kk[:, hh * LANE:(hh + 1) * LANE].astype(k_out.dtype)
        k_out[:, lo + LANE:lo + 2 * LANE] = kr
        v_out[:, lo:lo + LANE] = vv[:, hh * LANE:(hh + 1) * LANE].astype(v_out.dtype)
        v_out[:, lo + LANE:lo + 2 * LANE] = ones


def _pos_block(seqs, tm):
    def f(i):
        out = i
        for s, _ in seqs:
            out = jnp.where(i >= s // tm, i - s // tm, out)
        return out
    return f


def _attn_body(q_ref, k_ref, v_ref, o_ref, m_scr, acc_scr, *, nkv, heads, qw, vd):
    ki = pl.program_id(3)

    @pl.when(ki == 0)
    def _():
        m_scr[...] = jnp.full_like(m_scr, -jnp.inf)
        acc_scr[...] = jnp.zeros_like(acc_scr)

    tkv = k_ref.shape[0]
    for hh in range(heads):
        q = q_ref[:, hh * qw:(hh + 1) * qw]
        k = k_ref[:, hh * qw:(hh + 1) * qw]
        v = v_ref[:, hh * 2 * vd:(hh + 1) * 2 * vd]
        s = lax.dot_general(q, k, (((1,), (1,)), ((), ())), preferred_element_type=F32)
        m_prev = m_scr[hh]
        m_new = jnp.maximum(m_prev, jnp.broadcast_to(jnp.max(s, axis=-1, keepdims=True), m_prev.shape))
        alpha = jnp.exp2(m_prev - m_new)
        p = jnp.exp2(s - jnp.concatenate([m_new] * (tkv // LANE), axis=1))
        pv = jnp.dot(p.astype(v.dtype), v, preferred_element_type=F32)
        acc_scr[hh] = jnp.concatenate([alpha] * (2 * vd // LANE), axis=1) * acc_scr[hh] + pv
        m_scr[hh] = m_new

    @pl.when(ki == nkv - 1)
    def _():
        for hh in range(heads):
            o_ref[:, hh * vd:(hh + 1) * vd] = (acc_scr[hh, :, :vd] / acc_scr[hh, :, vd:]).astype(o_ref.dtype)


def _attention(q, k, v, row0, nseq, slen, cfg):
    nh, vd, qw = cfg.mla_heads, cfg.vdim, cfg.qk_pad
    assert vd == LANE
    hp = min(cfg.attn_heads_per_step, nh)
    tq, tkv = min(cfg.tq, slen), min(cfg.tkv, slen)
    assert slen % tq == 0 and slen % tkv == 0 and row0 % tq == 0 and row0 % tkv == 0 and nh % hp == 0
    nq, nkv = slen // tq, slen // tkv
    q0, k0 = row0 // tq, row0 // tkv
    return pl.pallas_call(
        functools.partial(_attn_body, nkv=nkv, heads=hp, qw=qw, vd=vd),
        grid=(nseq, nh // hp, nq, nkv),
        in_specs=[pl.BlockSpec((tq, hp * qw), lambda b, h, qi, ki: (q0 + b * nq + qi, h)),
                  pl.BlockSpec((tkv, hp * qw), lambda b, h, qi, ki: (k0 + b * nkv + ki, h)),
                  pl.BlockSpec((tkv, hp * 2 * vd), lambda b, h, qi, ki: (k0 + b * nkv + ki, h))],
        out_specs=pl.BlockSpec((tq, hp * vd), lambda b, h, qi, ki: (b * nq + qi, h)),
        out_shape=jax.ShapeDtypeStruct((nseq * slen, nh * vd), BF16),
        scratch_shapes=[pltpu.VMEM((hp, tq, LANE), F32), pltpu.VMEM((hp, tq, 2 * vd), F32)],
        compiler_params=_params(("parallel", "parallel", "parallel", "arbitrary")),
        name="mla_attention",
    )(q, k, v)


def _mla_mixer(xf, xb, w_in, q_norm_g, kv_norm_g, w_uq, w_ukv, w_out, seqs, cfg):
    t, d = xf.shape
    nh, nope, rope, vd = cfg.mla_heads, cfg.nope, cfg.rope, cfg.vdim
    qr, kvr = cfg.q_rank, cfg.kv_rank
    half = rope // 2
    assert nope == LANE and vd == LANE and rope <= LANE
    qw = cfg.qk_pad
    tm = min(cfg.mm_mla_in[0], t)
    max_len = max(n for _, n in seqs)
    pos = _pos_block(seqs, tm)

    w_in_p = jnp.pad(w_in, ((0, 0), (0, LANE - rope))).astype(BF16)
    w_uq_p = jnp.pad(w_uq.reshape(qr, nh, nope + rope), ((0, 0), (0, 0), (0, qw - nope - rope)))
    w_uq_p = w_uq_p.reshape(qr, nh * qw).astype(BF16)
    w_ukv_h = w_ukv.reshape(kvr, nh, nope + vd)
    w_uk = w_ukv_h[:, :, :nope].reshape(kvr, nh * nope).astype(BF16)
    w_uv = w_ukv_h[:, :, nope:].reshape(kvr, nh * vd).astype(BF16)

    n_in = qr + kvr + LANE
    cq, ckv, kr = _matmul(
        xb, w_in_p, tm=tm, tn=n_in, tk=cfg.mm_mla_in[2],
        epilogue=functools.partial(_ep_mla_in, q_rank=qr, kv_rank=kvr, eps=cfg.rms_eps),
        out_shape=[jax.ShapeDtypeStruct((t, qr), BF16), jax.ShapeDtypeStruct((t, kvr), BF16),
                   jax.ShapeDtypeStruct((t, LANE), F32)],
        out_specs=[pl.BlockSpec((tm, qr), lambda i, j, k: (i, 0)),
                   pl.BlockSpec((tm, kvr), lambda i, j, k: (i, 0)),
                   pl.BlockSpec((tm, LANE), lambda i, j, k: (i, 0))],
        extra=(q_norm_g.reshape(1, qr), kv_norm_g.reshape(1, kvr)),
        extra_specs=(pl.BlockSpec((1, qr), lambda i, j, k: (0, 0)),
                     pl.BlockSpec((1, kvr), lambda i, j, k: (0, 0))),
        name="mla_in_proj")

    scale = (nope + rope) ** -0.5 * LOG2_E
    q_tabs = _rope_tables(max_len, cfg, scale)
    k_tabs = _rope_tables(max_len, cfg, 1.0)
    tn = min(cfg.mm_q_up[1], nh * qw)
    tab3 = pl.BlockSpec((tm, LANE), lambda i, j, k: (pos(i), 0))
    q = _matmul(cq, w_uq_p, tm=tm, tn=tn, tk=cfg.mm_q_up[2],
                epilogue=functools.partial(_ep_q_up, heads=tn // qw, scale=scale, half=half),
                out_shape=jax.ShapeDtypeStruct((t, nh * qw), BF16),
                out_specs=pl.BlockSpec((tm, tn), lambda i, j, k: (i, j)),
                extra=q_tabs, extra_specs=(tab3, tab3, tab3), name="mla_q_up")

    hb = min(cfg.kv_heads_per_step, nh)
    tab2 = pl.BlockSpec((tm, LANE), lambda i, j: (pos(i), 0))
    k, v = pl.pallas_call(
        functools.partial(_kv_up_body, heads=hb, half=half),
        grid=(t // tm, nh // hb),
        in_specs=[pl.BlockSpec((tm, kvr), lambda i, j: (i, 0)),
                  pl.BlockSpec((kvr, hb * nope), lambda i, j: (0, j)),
                  pl.BlockSpec((kvr, hb * vd), lambda i, j: (0, j)),
                  pl.BlockSpec((tm, LANE), lambda i, j: (i, 0)),
                  tab2, tab2, tab2],
        out_specs=[pl.BlockSpec((tm, hb * qw), lambda i, j: (i, j)),
                   pl.BlockSpec((tm, hb * 2 * vd), lambda i, j: (i, j))],
        out_shape=[jax.ShapeDtypeStruct((t, nh * qw), BF16), jax.ShapeDtypeStruct((t, nh * 2 * vd), BF16)],
        compiler_params=_params(("parallel", "parallel")),
        name="mla_kv_up",
    )(ckv, w_uk, w_uv, kr, *k_tabs)

    outs = []
    idx = 0
    while idx < len(seqs):
        row0, slen = seqs[idx]
        nseq = 1
        while idx + nseq < len(seqs) and seqs[idx + nseq] == (row0 + nseq * slen, slen):
            nseq += 1
        outs.append(_attention(q, k, v, row0, nseq, slen, cfg))
        idx += nseq
    o = outs[0] if len(outs) == 1 else jnp.concatenate(outs, axis=0)
    return _mm_residual(o, w_out.astype(BF16), xf, cfg.mm_wide, cfg, "mla_out_proj")


def _ffn_up_body(x_ref, w_ref, halo_ref, cw_ref, cb_ref, out_ref, *, tm, half, seq_starts, seq_ends):
    r0 = pl.program_id(0) * tm
    at_start = functools.reduce(jnp.logical_or, [r0 == s for s in seq_starts])
    at_end = functools.reduce(jnp.logical_or, [r0 + tm == e for e in seq_ends])
    h = jnp.dot(x_ref[...], w_ref[...], preferred_element_type=F32)
    prev_row = jnp.where(at_start, 0.0, halo_ref[0:1, :])
    next_row = jnp.where(at_end, 0.0, halo_ref[1:2, :])
    h_prev = jnp.concatenate([prev_row, h[:tm - 1, :]], axis=0)
    h_next = jnp.concatenate([h[1:, :], next_row], axis=0)
    cw = cw_ref[...]
    y = h_prev * cw[0:1, :] + h * cw[1:2, :] + h_next * cw[2:3, :] + cb_ref[...]
    g, u = y[:, :half], y[:, half:]
    out_ref[...] = (g * jax.nn.sigmoid(g) * u).astype(out_ref.dtype)


def _conv_ffn(xf, xb, w_up, conv_w, conv_b, w_down, seqs, cfg):
    t = xb.shape[0]
    f, fp, cbw = cfg.d_ff, cfg.d_ff_pad, cfg.ff_block
    pad = fp - f
    nb = fp // cbw
    assert fp % cbw == 0

    def interleave(a):
        lead = [(0, 0)] * (a.ndim - 1)
        halves = [jnp.pad(h, lead + [(0, pad)]).reshape(*a.shape[:-1], nb, 1, cbw) for h in (a[..., :f], a[..., f:])]
        return jnp.concatenate(halves, axis=-2).reshape(*a.shape[:-1], 2 * fp)

    w_up_p = interleave(w_up).astype(BF16)
    w_down_p = jnp.pad(w_down, ((0, pad), (0, 0))).astype(BF16)
    cw, cb = interleave(conv_w), interleave(conv_b.reshape(1, -1))

    tm, _, tk = cfg.mm_wide
    tm = min(tm, t)
    nt = t // tm
    halo_idx = np.array([[max(i * tm - 1, 0), min((i + 1) * tm, t - 1)] * 4 for i in range(nt)], np.int32).reshape(-1)
    h_halo = _mm_plain(xb[halo_idx], w_up_p, (nt * 8, cfg.mm_halo_tn, tk), F32, "ffn_up_halo")
    d = xb.shape[1]
    assert tk == d
    act = pl.pallas_call(
        functools.partial(_ffn_up_body, tm=tm, half=cbw,
                          seq_starts=tuple(s for s, _ in seqs), seq_ends=tuple(s + n for s, n in seqs)),
        grid=(nt, nb),
        in_specs=[pl.BlockSpec((tm, d), lambda i, j: (i, 0)),
                  pl.BlockSpec((d, 2 * cbw), lambda i, j: (0, j)),
                  pl.BlockSpec((8, 2 * cbw), lambda i, j: (i, j)),
                  pl.BlockSpec((3, 2 * cbw), lambda i, j: (0, j)),
                  pl.BlockSpec((1, 2 * cbw), lambda i, j: (0, j))],
        out_specs=pl.BlockSpec((tm, cbw), lambda i, j: (i, j)),
        out_shape=jax.ShapeDtypeStruct((t, fp), BF16),
        compiler_params=_params(("parallel", "parallel")),
        name="ffn_up_conv_gate",
    )(xb, w_up_p, h_halo, cw, cb)
    return _mm_residual(act, w_down_p, xf, cfg.mm_down, cfg, "ffn_down")


def _trunk(x, seqs, cfg, ml_w_in, ml_b_gates, ml_norm_g, ml_w_out, mla_w_in, mla_q_norm_g, mla_kv_norm_g,
           mla_w_uq, mla_w_ukv, mla_w_out, ffn_w_up, ffn_conv_w, ffn_conv_b, ffn_w_down,
           ln1_g, ln1_b, ln2_g, ln2_b):
    xf, xb = x, x.astype(BF16)
    for i in range(cfg.depth):
        j = i // 2
        if i % 2 == 0:
            y = _mlstm_mixer(xf, xb, ml_w_in[j], ml_b_gates[j], ml_norm_g[j], ml_w_out[j], seqs, cfg)
        else:
            y = _mla_mixer(xf, xb, mla_w_in[j], mla_q_norm_g[j], mla_kv_norm_g[j],
                           mla_w_uq[j], mla_w_ukv[j], mla_w_out[j], seqs, cfg)
        xf, xb = _layer_norm(y, ln1_g[i], ln1_b[i], cfg)
        y = _conv_ffn(xf, xb, ffn_w_up[i], ffn_conv_w[i], ffn_conv_b[i], ffn_w_down[i], seqs, cfg)
        xf, xb = _layer_norm(y, ln2_g[i], ln2_b[i], cfg)
    return xf


def _run(cfg, x_prompt, x_sample, *weights):
    d = x_prompt.shape[-1]
    shapes = [x_prompt.shape, x_sample.shape]
    seqs, row = [], 0
    for b, s, _ in shapes:
        for _ in range(b):
            seqs.append((row, s))
            row += s
    x = jnp.concatenate([x_prompt.reshape(-1, d), x_sample.reshape(-1, d)], axis=0)
    y = _trunk(x, tuple(seqs), cfg, *weights)
    n_prompt = shapes[0][0] * shapes[0][1]
    return y[:n_prompt].reshape(shapes[0]), y[n_prompt:].reshape(shapes[1])


def kernel(x_prompt, x_sample, ml_w_in, ml_b_gates, ml_norm_g, ml_w_out, mla_w_in, mla_q_norm_g, mla_kv_norm_g, mla_w_uq, mla_w_ukv, mla_w_out, ffn_w_up, ffn_conv_w, ffn_conv_b, ffn_w_down, ln1_g, ln1_b, ln2_g, ln2_b):
    return _run(Cfg(), x_prompt, x_sample, ml_w_in, ml_b_gates, ml_norm_g, ml_w_out, mla_w_in, mla_q_norm_g,
                mla_kv_norm_g, mla_w_uq, mla_w_ukv, mla_w_out, ffn_w_up, ffn_conv_w, ffn_conv_b, ffn_w_down,
                ln1_g, ln1_b, ln2_g, ln2_b)
```

```python
import functools
from typing import NamedTuple

import jax
import jax.numpy as jnp
import numpy as np
from jax import lax
from jax.experimental import pallas as pl
from jax.experimental.pallas import tpu as pltpu

F32 = jnp.float32
BF16 = jnp.bfloat16

LANE = 128
MXU_DIM = 256
LOG2_E = 1.4426950408889634
VMEM_LIMIT = 56 * 1024 * 1024


class Cfg(NamedTuple):
    d_model: int = 4096
    depth: int = 4
    ml_heads: int = 8
    ml_dqk: int = 256
    ml_dv: int = 512
    ml_chunk: int = 512
    ml_m_init: float = -1e30
    mla_heads: int = 32
    nope: int = 128
    rope: int = 64
    vdim: int = 128
    q_rank: int = 1024
    kv_rank: int = 512
    rope_base: float = 10000.0
    d_ff: int = 11008
    d_ff_pad: int = 11264
    ln_eps: float = 1e-5
    rms_eps: float = 1e-6
    mm_wide: tuple = (1024, 1024, 4096)
    mm_down: tuple = (1024, 1024, 2816)
    mm_mla_in: tuple = (1024, 0, 512)
    mm_q_up: tuple = (1024, 1024, 1024)
    mm_halo_tn: int = 1024
    ff_block: int = 512
    t_row: int = 256
    tq: int = 512
    tkv: int = 2048
    attn_heads_per_step: int = 8
    kv_heads_per_step: int = 4

    @property
    def dn_alpha(self):
        return (2 * self.depth) ** 0.25

    @property
    def qk_pad(self):
        return 2 * LANE


def _params(sem, vmem=VMEM_LIMIT):
    return pltpu.CompilerParams(dimension_semantics=sem, vmem_limit_bytes=vmem)


def _mm_body(*refs, nk, n_extra, epilogue):
    x_ref, w_ref = refs[0], refs[1]
    extra = refs[2:2 + n_extra]
    ij, k = (pl.program_id(0), pl.program_id(1)), pl.program_id(2)
    part = jnp.dot(x_ref[...], w_ref[...], preferred_element_type=F32)
    if nk == 1:
        epilogue(part, extra, refs[2 + n_extra:], ij)
        return
    outs, acc_ref = refs[2 + n_extra:-1], refs[-1]

    @pl.when(k == 0)
    def _():
        acc_ref[...] = part

    @pl.when(jnp.logical_and(k > 0, k < nk - 1))
    def _():
        acc_ref[...] += part

    @pl.when(k == nk - 1)
    def _():
        epilogue(acc_ref[...] + part, extra, outs, ij)


def _matmul(x, w, *, tm, tn, tk, epilogue, out_shape, out_specs, extra=(), extra_specs=(), name):
    m, kdim = x.shape
    n = w.shape[1]
    tm, tn, tk = min(tm, m), min(tn, n), min(tk, kdim)
    assert m % tm == 0 and n % tn == 0 and kdim % tk == 0, (x.shape, w.shape, tm, tn, tk)
    nk = kdim // tk
    return pl.pallas_call(
        functools.partial(_mm_body, nk=nk, n_extra=len(extra), epilogue=epilogue),
        grid=(m // tm, n // tn, nk),
        in_specs=[pl.BlockSpec((tm, tk), lambda i, j, k: (i, k)),
                  pl.BlockSpec((tk, tn), lambda i, j, k: (k, j)),
                  *extra_specs],
        out_specs=out_specs,
        out_shape=out_shape,
        scratch_shapes=[] if nk == 1 else [pltpu.VMEM((tm, tn), F32)],
        compiler_params=_params(("parallel", "parallel", "arbitrary")),
        name=name,
    )(x, w, *extra)


def _ep_store(acc, extra, outs, ij):
    outs[0][...] = acc.astype(outs[0].dtype)


def _ep_residual(acc, extra, outs, ij, *, alpha):
    outs[0][...] = alpha * extra[0][...] + acc


def _ep_bias(acc, extra, outs, ij):
    outs[0][...] = acc + extra[0][...]


def _ep_sigmoid_from(acc, extra, outs, ij, *, first_gate_tile):
    @pl.when(ij[1] < first_gate_tile)
    def _():
        outs[0][...] = acc.astype(outs[0].dtype)

    @pl.when(ij[1] >= first_gate_tile)
    def _():
        outs[0][...] = jax.nn.sigmoid(acc).astype(outs[0].dtype)


def _mm_plain(x, w, tiles, out_dtype, name):
    m, n = x.shape[0], w.shape[1]
    tm, tn, tk = tiles
    tm, tn = min(tm, m), min(tn, n)
    return _matmul(x, w, tm=tm, tn=tn, tk=tk, epilogue=_ep_store,
                   out_shape=jax.ShapeDtypeStruct((m, n), out_dtype),
                   out_specs=pl.BlockSpec((tm, tn), lambda i, j, k: (i, j)), name=name)


def _mm_residual(x, w, resid, tiles, cfg, name):
    m, n = x.shape[0], w.shape[1]
    tm, tn, tk = tiles
    tm, tn = min(tm, m), min(tn, n)
    spec = pl.BlockSpec((tm, tn), lambda i, j, k: (i, j))
    return _matmul(x, w, tm=tm, tn=tn, tk=tk,
                   epilogue=functools.partial(_ep_residual, alpha=cfg.dn_alpha),
                   out_shape=jax.ShapeDtypeStruct((m, n), F32), out_specs=spec,
                   extra=(resid,), extra_specs=(spec,), name=name)


def _ln_body(y_ref, g_ref, b_ref, xf_ref, xb_ref, *, eps):
    y = y_ref[...]
    mu = jnp.mean(y, axis=-1, keepdims=True)
    d = y - mu
    var = jnp.mean(d * d, axis=-1, keepdims=True)
    out = d * lax.rsqrt(var + eps) * g_ref[...] + b_ref[...]
    xf_ref[...] = out
    xb_ref[...] = out.astype(BF16)


def _layer_norm(y, g, b, cfg):
    t, d = y.shape
    tr = min(cfg.t_row, t)
    row = pl.BlockSpec((tr, d), lambda i: (i, 0))
    vec = pl.BlockSpec((1, d), lambda i: (0, 0))
    return pl.pallas_call(
        functools.partial(_ln_body, eps=cfg.ln_eps),
        grid=(t // tr,),
        in_specs=[row, vec, vec],
        out_specs=[row, row],
        out_shape=[jax.ShapeDtypeStruct((t, d), F32), jax.ShapeDtypeStruct((t, d), BF16)],
        compiler_params=_params(("parallel",)),
        name="layer_norm",
    )(y, g.reshape(1, d), b.reshape(1, d))


def _split_specs(tr, d, na_tiles):
    return (pl.BlockSpec((tr, d), lambda i: (jnp.minimum(i, na_tiles - 1), 0)),
            pl.BlockSpec((tr, d), lambda i: (jnp.maximum(i - na_tiles, 0), 0)))


def _ln_split_body(y_ref, g_ref, b_ref, oa_ref, ob_ref, *, eps, na_tiles):
    i = pl.program_id(0)
    y = y_ref[...]
    mu = jnp.mean(y, axis=-1, keepdims=True)
    d = y - mu
    var = jnp.mean(d * d, axis=-1, keepdims=True)
    out = d * lax.rsqrt(var + eps) * g_ref[...] + b_ref[...]

    @pl.when(i < na_tiles)
    def _():
        oa_ref[...] = out

    @pl.when(i >= na_tiles)
    def _():
        ob_ref[...] = out


def _layer_norm_split(y, g, b, n_a, cfg):
    t, d = y.shape
    tr = min(cfg.t_row, t)
    assert n_a % tr == 0 and (t - n_a) % tr == 0 and 0 < n_a < t
    vec = pl.BlockSpec((1, d), lambda i: (0, 0))
    return pl.pallas_call(
        functools.partial(_ln_split_body, eps=cfg.ln_eps, na_tiles=n_a // tr),
        grid=(t // tr,),
        in_specs=[pl.BlockSpec((tr, d), lambda i: (i, 0)), vec, vec],
        out_specs=list(_split_specs(tr, d, n_a // tr)),
        out_shape=[jax.ShapeDtypeStruct((n_a, d), F32), jax.ShapeDtypeStruct((t - n_a, d), F32)],
        compiler_params=_params(("arbitrary",)),
        name="layer_norm_out",
    )(y, g.reshape(1, d), b.reshape(1, d))


def _stream_in_body(xa_ref, xb_ref, of_ref, ob_ref, *, na_tiles):
    i = pl.program_id(0)

    @pl.when(i < na_tiles)
    def _():
        of_ref[...] = xa_ref[...]
        ob_ref[...] = xa_ref[...].astype(BF16)

    @pl.when(i >= na_tiles)
    def _():
        of_ref[...] = xb_ref[...]
        ob_ref[...] = xb_ref[...].astype(BF16)


def _stream_in(xa, xb, cfg):
    (n_a, d), n_b = xa.shape, xb.shape[0]
    t = n_a + n_b
    tr = min(cfg.t_row, n_a, n_b)
    assert n_a % tr == 0 and n_b % tr == 0
    row = pl.BlockSpec((tr, d), lambda i: (i, 0))
    return pl.pallas_call(
        functools.partial(_stream_in_body, na_tiles=n_a // tr),
        grid=(t // tr,),
        in_specs=list(_split_specs(tr, d, n_a // tr)),
        out_specs=[row, row],
        out_shape=[jax.ShapeDtypeStruct((t, d), F32), jax.ShapeDtypeStruct((t, d), BF16)],
        compiler_params=_params(("arbitrary",)),
        name="stream_in",
    )(xa, xb)


def _mlstm_body(q_ref, k_ref, v_ref, g_ref, *rest, chunk, nc, reverse, reset_chunks, m_init, k_scale, combine_eps):
    if combine_eps is None:
        h_ref, c_scr, n_scr, m_scr = rest
    else:
        hf_ref, og_ref, ng_ref, h_ref, c_scr, n_scr, m_scr = rest
    c = pl.program_id(1)
    cc = nc - 1 - c if reverse else c

    @pl.when(functools.reduce(jnp.logical_or, [cc == r for r in reset_chunks]))
    def _():
        c_scr[...] = jnp.zeros_like(c_scr)
        n_scr[...] = jnp.zeros_like(n_scr)
        m_scr[...] = jnp.full_like(m_scr, m_init)

    gates = g_ref[0]
    ri, rf = (2, 3) if reverse else (0, 1)
    i_row = gates[ri:ri + 1, :]
    lf_row = jax.nn.log_sigmoid(gates[rf:rf + 1, :])

    row = lax.broadcasted_iota(jnp.int32, (chunk, chunk), 0)
    col = lax.broadcasted_iota(jnp.int32, (chunk, chunk), 1)
    visible = (col >= row) if reverse else (col <= row)
    visible_t = (row >= col) if reverse else (row <= col)
    b_col = jnp.sum(jnp.where(visible, lf_row, 0.0), axis=-1, keepdims=True)
    lf_col = jnp.sum(jnp.where(row == col, lf_row, 0.0), axis=-1, keepdims=True)
    b_row = jnp.sum(jnp.where(visible_t, lf_col, 0.0), axis=0, keepdims=True)

    m_prev = m_scr[0:1, 0:1]
    logd = jnp.where(visible, b_col - b_row + i_row, -jnp.inf)
    inter = b_col + m_prev
    m_row = jnp.maximum(jnp.max(logd, axis=-1, keepdims=True), inter)
    dmat = jnp.exp(logd - m_row)

    q = q_ref[...]
    k_f32 = k_ref[...].astype(F32) * k_scale
    k = k_f32.astype(BF16)
    v = v_ref[...]
    s = lax.dot_general(q, k, (((1,), (1,)), ((), ())), preferred_element_type=F32) * dmat
    w_inter = jnp.exp(inter - m_row)
    ct = c_scr[...]
    qc = jnp.dot(q, ct.astype(BF16), preferred_element_type=F32)
    num = jnp.dot(s.astype(BF16), v, preferred_element_type=F32) + w_inter * qc
    qn = jnp.dot(q, n_scr[...].astype(BF16), preferred_element_type=F32)[:, 0:1]
    den = jnp.sum(s, axis=-1, keepdims=True) + w_inter * qn
    h = num / jnp.maximum(jnp.abs(den), jnp.exp(-m_row))
    if combine_eps is None:
        h_ref[...] = h
    else:
        h = h + hf_ref[...]
        h = h * lax.rsqrt(jnp.mean(h * h, axis=-1, keepdims=True) + combine_eps)
        h_ref[...] = (h * ng_ref[...] * og_ref[...].astype(F32)).astype(h_ref.dtype)

    bl = jnp.sum(lf_row, axis=-1, keepdims=True)
    a_row = bl - b_row + i_row
    m_new = jnp.maximum(bl + m_prev, jnp.max(a_row, axis=-1, keepdims=True))
    decay = jnp.exp(bl + m_prev - m_new)
    wk_t = k_f32.T * jnp.exp(a_row - m_new)
    c_scr[...] = decay * ct + jnp.dot(wk_t.astype(BF16), v, preferred_element_type=F32)
    n_scr[...] = decay * n_scr[...] + jnp.sum(wk_t, axis=-1, keepdims=True)
    m_scr[...] = jnp.broadcast_to(m_new, m_scr.shape)


def _mlstm_scan(qkv, gates, seqs, cfg, *, reverse, h_other=None, norm_g=None):
    t = qkv.shape[0]
    nh, dqk, dv, chunk = cfg.ml_heads, cfg.ml_dqk, cfg.ml_dv, cfg.ml_chunk
    chunk = min(chunk, min(n for _, n in seqs))
    assert all(s % chunk == 0 and n % chunk == 0 for s, n in seqs)
    nc = t // chunk
    if reverse:
        resets = tuple((s + n) // chunk - 1 for s, n in seqs)
    else:
        resets = tuple(s // chunk for s, _ in seqs)

    def ci(c):
        return nc - 1 - c if reverse else c

    assert (2 * nh * dqk) % dv == 0
    v_off = 2 * nh * dqk // dv
    o_off = v_off + nh
    combine = h_other is not None
    h_spec = pl.BlockSpec((chunk, dv), lambda h, c: (ci(c), h))
    in_specs = [pl.BlockSpec((chunk, dqk), lambda h, c: (ci(c), h)),
                pl.BlockSpec((chunk, dqk), lambda h, c: (ci(c), nh + h)),
                pl.BlockSpec((chunk, dv), lambda h, c: (ci(c), v_off + h)),
                pl.BlockSpec((1, 8, chunk), lambda h, c: (h, 0, ci(c)))]
    operands = [qkv, qkv, qkv, gates]
    if combine:
        in_specs += [h_spec, pl.BlockSpec((chunk, dv), lambda h, c: (ci(c), o_off + h)),
                     pl.BlockSpec((1, dv), lambda h, c: (0, h))]
        operands += [h_other, qkv, norm_g.reshape(1, nh * dv)]
    return pl.pallas_call(
        functools.partial(_mlstm_body, chunk=chunk, nc=nc, reverse=reverse, reset_chunks=resets,
                          m_init=cfg.ml_m_init, k_scale=dqk ** -0.5,
                          combine_eps=cfg.rms_eps if combine else None),
        grid=(nh, nc),
        in_specs=in_specs,
        out_specs=h_spec,
        out_shape=jax.ShapeDtypeStruct((t, nh * dv), BF16 if combine else F32),
        scratch_shapes=[pltpu.VMEM((dqk, dv), F32), pltpu.VMEM((dqk, LANE), F32), pltpu.VMEM((8, LANE), F32)],
        compiler_params=_params(("parallel", "arbitrary")),
        name="mlstm_scan_bw" if reverse else "mlstm_scan_fw",
    )(*operands)


def _mlstm_mixer(xf, xb, w_in, b_gates, norm_g, w_out, seqs, cfg):
    t = xf.shape[0]
    nh, dqk, dv = cfg.ml_heads, cfg.ml_dqk, cfg.ml_dv
    qk_w, v_w = nh * dqk, nh * dv
    main_w = 2 * qk_w + 2 * v_w
    n_gates = 4 * nh
    w_main = w_in[:, :main_w].astype(BF16)
    w_gate = jnp.pad(w_in[:, main_w:], ((0, 0), (0, LANE - n_gates))).astype(BF16)
    b_gate = jnp.pad(b_gates, (0, LANE - n_gates)).reshape(1, LANE)

    tm, tn, tk = cfg.mm_wide
    tm, tn = min(tm, t), min(tn, v_w)
    assert (2 * qk_w + v_w) % tn == 0
    proj = _matmul(xb, w_main, tm=tm, tn=tn, tk=tk,
                   epilogue=functools.partial(_ep_sigmoid_from, first_gate_tile=(2 * qk_w + v_w) // tn),
                   out_shape=jax.ShapeDtypeStruct((t, main_w), BF16),
                   out_specs=pl.BlockSpec((tm, tn), lambda i, j, k: (i, j)), name="mlstm_in_proj")
    gates = _matmul(xb, w_gate, tm=tm, tn=LANE, tk=tk, epilogue=_ep_bias,
                    out_shape=jax.ShapeDtypeStruct((t, LANE), F32),
                    out_specs=pl.BlockSpec((tm, LANE), lambda i, j, k: (i, 0)),
                    extra=(b_gate,), extra_specs=(pl.BlockSpec((1, LANE), lambda i, j, k: (0, 0)),),
                    name="mlstm_gate_proj")
    g = gates[:, :n_gates].reshape(t, 4, nh).transpose(2, 1, 0)
    g = jnp.pad(g, ((0, 0), (0, 4), (0, 0)))
    h_fw = _mlstm_scan(proj, g, seqs, cfg, reverse=False)
    hact = _mlstm_scan(proj, g, seqs, cfg, reverse=True, h_other=h_fw, norm_g=norm_g)
    return _mm_residual(hact, w_out.astype(BF16), xf, cfg.mm_wide, cfg, "mlstm_out_proj")


def _rope_tables(max_len, cfg, scale):
    half = cfg.rope // 2
    inv_freq = jnp.power(cfg.rope_base, -jnp.arange(0, cfg.rope, 2, dtype=F32) / cfg.rope)
    ang = jnp.arange(max_len, dtype=F32)[:, None] * inv_freq[None, :]
    cos, sin = jnp.cos(ang), jnp.sin(ang)
    z = jnp.zeros((max_len, LANE - 2 * half), F32)
    zh = jnp.zeros((max_len, half), F32)
    cos_t = jnp.concatenate([cos, cos, z], axis=1) * scale
    sin_a = jnp.concatenate([-sin, zh, z], axis=1) * scale
    sin_b = jnp.concatenate([zh, sin, z], axis=1) * scale
    return cos_t, sin_a, sin_b


def _rotate(x, cos_t, sin_a, sin_b, half):
    return x * cos_t + pltpu.roll(x, LANE - half, 1) * sin_a + pltpu.roll(x, half, 1) * sin_b


def _ep_mla_in(acc, extra, outs, ij, *, q_rank, kv_rank, eps):
    gq_ref, gkv_ref = extra
    cq_ref, ckv_ref, kr_ref = outs

    def rms(x, g):
        return x * lax.rsqrt(jnp.mean(x * x, axis=-1, keepdims=True) + eps) * g

    cq_ref[...] = rms(acc[:, :q_rank], gq_ref[...]).astype(cq_ref.dtype)
    ckv_ref[...] = rms(acc[:, q_rank:q_rank + kv_rank], gkv_ref[...]).astype(ckv_ref.dtype)
    kr_ref[...] = acc[:, q_rank + kv_rank:]


def _ep_q_up(acc, extra, outs, ij, *, heads, scale, half):
    cos_ref, sa_ref, sb_ref = extra
    out = outs[0]
    for hh in range(heads):
        lo = hh * 2 * LANE
        out[:, lo:lo + LANE] = (acc[:, lo:lo + LANE] * scale).astype(out.dtype)
        r = acc[:, lo + LANE:lo + 2 * LANE]
        out[:, lo + LANE:lo + 2 * LANE] = _rotate(r, cos_ref[...], sa_ref[...], sb_ref[...], half).astype(out.dtype)


def _kv_up_body(x_ref, wk_ref, wv_ref, kr_ref, cos_ref, sa_ref, sb_ref, k_out, v_out, *, heads, half):
    x = x_ref[...]
    kk = jnp.dot(x, wk_ref[...], preferred_element_type=F32)
    vv = jnp.dot(x, wv_ref[...], preferred_element_type=F32)
    kr = _rotate(kr_ref[...], cos_ref[...], sa_ref[...], sb_ref[...], half).astype(k_out.dtype)
    ones = jnp.ones((x.shape[0], LANE), v_out.dtype)
    for hh in range(heads):
        lo = hh * 2 * LANE
        k_out[:, lo:lo + LANE] = kk[:, hh * LANE:(hh + 1) * LANE].astype(k_out.dtype)
        k_out[:, lo + LANE:lo + 2 * LANE] = kr
        v_out[:, lo:lo + LANE] = vv[:, hh * LANE:(hh + 1) * LANE].astype(v_out.dtype)
        v_out[:, lo + LANE:lo + 2 * LANE] = ones


def _pos_block(seqs, tm):
    def f(i):
        out = i
        for s, _ in seqs:
            out = jnp.where(i >= s // tm, i - s // tm, out)
        return out
    return f


def _attn_body(q_ref, k_ref, v_ref, o_ref, m_scr, acc_scr, *, nkv, heads, qw, vd):
    ki = pl.program_id(3)

    @pl.when(ki == 0)
    def _():
        m_scr[...] = jnp.full_like(m_scr, -jnp.inf)
        acc_scr[...] = jnp.zeros_like(acc_scr)

    tkv = k_ref.shape[0]
    for hh in range(heads):
        q = q_ref[:, hh * qw:(hh + 1) * qw]
        k = k_ref[:, hh * qw:(hh + 1) * qw]
        v = v_ref[:, hh * 2 * vd:(hh + 1) * 2 * vd]
        s = lax.dot_general(q, k, (((1,), (1,)), ((), ())), preferred_element_type=F32)
        m_prev = m_scr[hh]
        m_new = jnp.maximum(m_prev, jnp.broadcast_to(jnp.max(s, axis=-1, keepdims=True), m_prev.shape))
        alpha = jnp.exp2(m_prev - m_new)
        p = jnp.exp2(s - jnp.concatenate([m_new] * (tkv // LANE), axis=1))
        pv = jnp.dot(p.astype(v.dtype), v, preferred_element_type=F32)
        acc_scr[hh] = jnp.concatenate([alpha] * (2 * vd // LANE), axis=1) * acc_scr[hh] + pv
        m_scr[hh] = m_new

    @pl.when(ki == nkv - 1)
    def _():
        for hh in range(heads):
            o_ref[:, hh * vd:(hh + 1) * vd] = (acc_scr[hh, :, :vd] / acc_scr[hh, :, vd:]).astype(o_ref.dtype)


def _attention(q, k, v, row0, nseq, slen, cfg):
    nh, vd, qw = cfg.mla_heads, cfg.vdim, cfg.qk_pad
    assert vd == LANE
    hp = min(cfg.attn_heads_per_step, nh)
    tq, tkv = min(cfg.tq, slen), min(cfg.tkv, slen)
    assert slen % tq == 0 and slen % tkv == 0 and row0 % tq == 0 and row0 % tkv == 0 and nh % hp == 0
    nq, nkv = slen // tq, slen // tkv
    q0, k0 = row0 // tq, row0 // tkv
    return pl.pallas_call(
        functools.partial(_attn_body, nkv=nkv, heads=hp, qw=qw, vd=vd),
        grid=(nseq, nh // hp, nq, nkv),
        in_specs=[pl.BlockSpec((tq, hp * qw), lambda b, h, qi, ki: (q0 + b * nq + qi, h)),
                  pl.BlockSpec((tkv, hp * qw), lambda b, h, qi, ki: (k0 + b * nkv + ki, h)),
                  pl.BlockSpec((tkv, hp * 2 * vd), lambda b, h, qi, ki: (k0 + b * nkv + ki, h))],
        out_specs=pl.BlockSpec((tq, hp * vd), lambda b, h, qi, ki: (b * nq + qi, h)),
        out_shape=jax.ShapeDtypeStruct((nseq * slen, nh * vd), BF16),
        scratch_shapes=[pltpu.VMEM((hp, tq, LANE), F32), pltpu.VMEM((hp, tq, 2 * vd), F32)],
        compiler_params=_params(("parallel", "parallel", "parallel", "arbitrary")),
        name="mla_attention",
    )(q, k, v)


def _mla_mixer(xf, xb, w_in, q_norm_g, kv_norm_g, w_uq, w_ukv, w_out, seqs, cfg):
    t, d = xf.shape
    nh, nope, rope, vd = cfg.mla_heads, cfg.nope, cfg.rope, cfg.vdim
    qr, kvr = cfg.q_rank, cfg.kv_rank
    half = rope // 2
    assert nope == LANE and vd == LANE and rope <= LANE
    qw = cfg.qk_pad
    tm = min(cfg.mm_mla_in[0], t)
    max_len = max(n for _, n in seqs)
    pos = _pos_block(seqs, tm)

    w_in_p = jnp.pad(w_in, ((0, 0), (0, LANE - rope))).astype(BF16)
    w_uq_p = jnp.pad(w_uq.reshape(qr, nh, nope + rope), ((0, 0), (0, 0), (0, qw - nope - rope)))
    w_uq_p = w_uq_p.reshape(qr, nh * qw).astype(BF16)
    w_ukv_h = w_ukv.reshape(kvr, nh, nope + vd)
    w_uk = w_ukv_h[:, :, :nope].reshape(kvr, nh * nope).astype(BF16)
    w_uv = w_ukv_h[:, :, nope:].reshape(kvr, nh * vd).astype(BF16)

    n_in = qr + kvr + LANE
    cq, ckv, kr = _matmul(
        xb, w_in_p, tm=tm, tn=n_in, tk=cfg.mm_mla_in[2],
        epilogue=functools.partial(_ep_mla_in, q_rank=qr, kv_rank=kvr, eps=cfg.rms_eps),
        out_shape=[jax.ShapeDtypeStruct((t, qr), BF16), jax.ShapeDtypeStruct((t, kvr), BF16),
                   jax.ShapeDtypeStruct((t, LANE), F32)],
        out_specs=[pl.BlockSpec((tm, qr), lambda i, j, k: (i, 0)),
                   pl.BlockSpec((tm, kvr), lambda i, j, k: (i, 0)),
                   pl.BlockSpec((tm, LANE), lambda i, j, k: (i, 0))],
        extra=(q_norm_g.reshape(1, qr), kv_norm_g.reshape(1, kvr)),
        extra_specs=(pl.BlockSpec((1, qr), lambda i, j, k: (0, 0)),
                     pl.BlockSpec((1, kvr), lambda i, j, k: (0, 0))),
        name="mla_in_proj")

    scale = (nope + rope) ** -0.5 * LOG2_E
    q_tabs = _rope_tables(max_len, cfg, scale)
    k_tabs = _rope_tables(max_len, cfg, 1.0)
    tn = min(cfg.mm_q_up[1], nh * qw)
    tab3 = pl.BlockSpec((tm, LANE), lambda i, j, k: (pos(i), 0))
    q = _matmul(cq, w_uq_p, tm=tm, tn=tn, tk=cfg.mm_q_up[2],
                epilogue=functools.partial(_ep_q_up, heads=tn // qw, scale=scale, half=half),
                out_shape=jax.ShapeDtypeStruct((t, nh * qw), BF16),
                out_specs=pl.BlockSpec((tm, tn), lambda i, j, k: (i, j)),
                extra=q_tabs, extra_specs=(tab3, tab3, tab3), name="mla_q_up")

    hb = min(cfg.kv_heads_per_step, nh)
    tab2 = pl.BlockSpec((tm, LANE), lambda i, j: (pos(i), 0))
    k, v = pl.pallas_call(
        functools.partial(_kv_up_body, heads=hb, half=half),
        grid=(t // tm, nh // hb),
        in_specs=[pl.BlockSpec((tm, kvr), lambda i, j: (i, 0)),
                  pl.BlockSpec((kvr, hb * nope), lambda i, j: (0, j)),
                  pl.BlockSpec((kvr, hb * vd), lambda i, j: (0, j)),
                  pl.BlockSpec((tm, LANE), lambda i, j: (i, 0)),
                  tab2, tab2, tab2],
        out_specs=[pl.BlockSpec((tm, hb * qw), lambda i, j: (i, j)),
                   pl.BlockSpec((tm, hb * 2 * vd), lambda i, j: (i, j))],
        out_shape=[jax.ShapeDtypeStruct((t, nh * qw), BF16), jax.ShapeDtypeStruct((t, nh * 2 * vd), BF16)],
        compiler_params=_params(("parallel", "parallel")),
        name="mla_kv_up",
    )(ckv, w_uk, w_uv, kr, *k_tabs)

    outs = []
    idx = 0
    while idx < len(seqs):
        row0, slen = seqs[idx]
        nseq = 1
        while idx + nseq < len(seqs) and seqs[idx + nseq] == (row0 + nseq * slen, slen):
            nseq += 1
        outs.append(_attention(q, k, v, row0, nseq, slen, cfg))
        idx += nseq
    o = outs[0] if len(outs) == 1 else jnp.concatenate(outs, axis=0)
    return _mm_residual(o, w_out.astype(BF16), xf, cfg.mm_wide, cfg, "mla_out_proj")


def _ffn_up_body(x_ref, wg_ref, wu_ref, hg_ref, hu_ref, cwg_ref, cwu_ref, cbg_ref, cbu_ref, out_ref, *,
                 tm, seq_starts, seq_ends):
    r0 = pl.program_id(0) * tm
    at_start = functools.reduce(jnp.logical_or, [r0 == s for s in seq_starts])
    at_end = functools.reduce(jnp.logical_or, [r0 + tm == e for e in seq_ends])
    x = x_ref[...]

    def conv(w_ref, halo_ref, cw_ref, cb_ref):
        h = jnp.dot(x, w_ref[...], preferred_element_type=F32)
        prev_row = jnp.where(at_start, 0.0, halo_ref[0:1, :])
        next_row = jnp.where(at_end, 0.0, halo_ref[1:2, :])
        h_prev = jnp.concatenate([prev_row, h[:tm - 1, :]], axis=0)
        h_next = jnp.concatenate([h[1:, :], next_row], axis=0)
        cw = cw_ref[...]
        return h_prev * cw[0:1, :] + h * cw[1:2, :] + h_next * cw[2:3, :] + cb_ref[...]

    g = conv(wg_ref, hg_ref, cwg_ref, cbg_ref)
    u = conv(wu_ref, hu_ref, cwu_ref, cbu_ref)
    out_ref[...] = (g * jax.nn.sigmoid(g) * u).astype(out_ref.dtype)


def _conv_ffn(xf, xb, w_up, conv_w, conv_b, w_down, seqs, cfg):
    t, d = xb.shape
    f, fp, cbw = cfg.d_ff, cfg.d_ff_pad, cfg.ff_block
    pad = fp - f
    nb = fp // cbw
    assert fp % cbw == 0

    def pad_halves(a):
        lead = [(0, 0)] * (a.ndim - 1)
        return jnp.concatenate([jnp.pad(a[..., :f], lead + [(0, pad)]), jnp.pad(a[..., f:], lead + [(0, pad)])], -1)

    w_up_p = pad_halves(w_up.astype(BF16))
    w_down_p = jnp.pad(w_down.astype(BF16), ((0, pad), (0, 0)))
    cw, cb = pad_halves(conv_w), pad_halves(conv_b.reshape(1, -1))

    tm, _, tk = cfg.mm_wide
    tm = min(tm, t)
    nt = t // tm
    assert tk == d
    halo_idx = np.array([[max(i * tm - 1, 0), min((i + 1) * tm, t - 1)] * 4 for i in range(nt)], np.int32).reshape(-1)
    h_halo = _mm_plain(xb[halo_idx], w_up_p, (nt * 8, cfg.mm_halo_tn, tk), F32, "ffn_up_halo")

    def gate_cols(rows):
        return pl.BlockSpec((rows, cbw), lambda i, j: (0, j))

    def up_cols(rows):
        return pl.BlockSpec((rows, cbw), lambda i, j: (0, nb + j))

    act = pl.pallas_call(
        functools.partial(_ffn_up_body, tm=tm,
                          seq_starts=tuple(s for s, _ in seqs), seq_ends=tuple(s + n for s, n in seqs)),
        grid=(nt, nb),
        in_specs=[pl.BlockSpec((tm, d), lambda i, j: (i, 0)),
                  gate_cols(d), up_cols(d),
                  pl.BlockSpec((8, cbw), lambda i, j: (i, j)), pl.BlockSpec((8, cbw), lambda i, j: (i, nb + j)),
                  gate_cols(3), up_cols(3), gate_cols(1), up_cols(1)],
        out_specs=pl.BlockSpec((tm, cbw), lambda i, j: (i, j)),
        out_shape=jax.ShapeDtypeStruct((t, fp), BF16),
        compiler_params=_params(("parallel", "parallel")),
        name="ffn_up_conv_gate",
    )(xb, w_up_p, w_up_p, h_halo, h_halo, cw, cw, cb, cb)
    return _mm_residual(act, w_down_p, xf, cfg.mm_down, cfg, "ffn_down")


def _trunk(xf, xb, n_first, seqs, cfg, ml_w_in, ml_b_gates, ml_norm_g, ml_w_out, mla_w_in, mla_q_norm_g,
           mla_kv_norm_g, mla_w_uq, mla_w_ukv, mla_w_out, ffn_w_up, ffn_conv_w, ffn_conv_b, ffn_w_down,
           ln1_g, ln1_b, ln2_g, ln2_b):
    for i in range(cfg.depth):
        j = i // 2
        if i % 2 == 0:
            y = _mlstm_mixer(xf, xb, ml_w_in[j], ml_b_gates[j], ml_norm_g[j], ml_w_out[j], seqs, cfg)
        else:
            y = _mla_mixer(xf, xb, mla_w_in[j], mla_q_norm_g[j], mla_kv_norm_g[j],
                           mla_w_uq[j], mla_w_ukv[j], mla_w_out[j], seqs, cfg)
        xf, xb = _layer_norm(y, ln1_g[i], ln1_b[i], cfg)
        y = _conv_ffn(xf, xb, ffn_w_up[i], ffn_conv_w[i], ffn_conv_b[i], ffn_w_down[i], seqs, cfg)
        if i == cfg.depth - 1:
            return _layer_norm_split(y, ln2_g[i], ln2_b[i], n_first, cfg)
        xf, xb = _layer_norm(y, ln2_g[i], ln2_b[i], cfg)


def _run(cfg, x_prompt, x_sample, *weights):
    d = x_prompt.shape[-1]
    shapes = [x_prompt.shape, x_sample.shape]
    seqs, row = [], 0
    for b, s, _ in shapes:
        for _ in range(b):
            seqs.append((row, s))
            row += s
    xf, xb = _stream_in(x_prompt.reshape(-1, d), x_sample.reshape(-1, d), cfg)
    n_prompt = shapes[0][0] * shapes[0][1]
    y_prompt, y_sample = _trunk(xf, xb, n_prompt, tuple(seqs), cfg, *weights)
    return y_prompt.reshape(shapes[0]), y_sample.reshape(shapes[1])


def kernel(x_prompt, x_sample, ml_w_in, ml_b_gates, ml_norm_g, ml_w_out, mla_w_in, mla_q_norm_g, mla_kv_norm_g, mla_w_uq, mla_w_ukv, mla_w_out, ffn_w_up, ffn_conv_w, ffn_conv_b, ffn_w_down, ln1_g, ln1_b, ln2_g, ln2_b):
    return _run(Cfg(), x_prompt, x_sample, ml_w_in, ml_b_gates, ml_norm_g, ml_w_out, mla_w_in, mla_q_norm_g,
                mla_kv_norm_g, mla_w_uq, mla_w_ukv, mla_w_out, ffn_w_up, ffn_conv_w, ffn_conv_b, ffn_w_down,
                ln1_g, ln1_b, ln2_g, ln2_b)
```

```python
import functools
from typing import NamedTuple

import jax
import jax.numpy as jnp
import numpy as np
from jax import lax
from jax.experimental import pallas as pl
from jax.experimental.pallas import tpu as pltpu

F32 = jnp.float32
BF16 = jnp.bfloat16

LANE = 128
MXU_DIM = 256
LOG2_E = 1.4426950408889634
VMEM_LIMIT = 56 * 1024 * 1024


class Cfg(NamedTuple):
    d_model: int = 4096
    depth: int = 4
    ml_heads: int = 8
    ml_dqk: int = 256
    ml_dv: int = 512
    ml_chunk: int = 512
    ml_m_init: float = -1e30
    mla_heads: int = 32
    nope: int = 128
    rope: int = 64
    vdim: int = 128
    q_rank: int = 1024
    kv_rank: int = 512
    rope_base: float = 10000.0
    d_ff: int = 11008
    d_ff_pad: int = 11264
    ln_eps: float = 1e-5
    rms_eps: float = 1e-6
    mm_wide: tuple = (1024, 1024, 4096)
    mm_down: tuple = (1024, 1024, 2816)
    mm_mla_in: tuple = (512, 0, 4096)
    mm_q_up: tuple = (1024, 1024, 1024)
    mm_halo_tn: int = 1024
    ff_block: int = 512
    t_row: int = 256
    tq: int = 1024
    tkv: int = 1024
    attn_heads_per_step: int = 8
    kv_heads_per_step: int = 4

    @property
    def dn_alpha(self):
        return (2 * self.depth) ** 0.25

    @property
    def qk_pad(self):
        return 2 * LANE


def _params(sem, vmem=VMEM_LIMIT):
    return pltpu.CompilerParams(dimension_semantics=sem, vmem_limit_bytes=vmem)


def _mm_body(*refs, nk, n_extra, epilogue):
    x_ref, w_ref = refs[0], refs[1]
    extra = refs[2:2 + n_extra]
    ij, k = (pl.program_id(0), pl.program_id(1)), pl.program_id(2)
    part = jnp.dot(x_ref[...], w_ref[...], preferred_element_type=F32)
    if nk == 1:
        epilogue(part, extra, refs[2 + n_extra:], ij)
        return
    outs, acc_ref = refs[2 + n_extra:-1], refs[-1]

    @pl.when(k == 0)
    def _():
        acc_ref[...] = part

    @pl.when(jnp.logical_and(k > 0, k < nk - 1))
    def _():
        acc_ref[...] += part

    @pl.when(k == nk - 1)
    def _():
        epilogue(acc_ref[...] + part, extra, outs, ij)


def _matmul(x, w, *, tm, tn, tk, epilogue, out_shape, out_specs, extra=(), extra_specs=(), name):
    m, kdim = x.shape
    n = w.shape[1]
    tm, tn, tk = min(tm, m), min(tn, n), min(tk, kdim)
    assert m % tm == 0 and n % tn == 0 and kdim % tk == 0, (x.shape, w.shape, tm, tn, tk)
    nk = kdim // tk
    return pl.pallas_call(
        functools.partial(_mm_body, nk=nk, n_extra=len(extra), epilogue=epilogue),
        grid=(m // tm, n // tn, nk),
        in_specs=[pl.BlockSpec((tm, tk), lambda i, j, k: (i, k)),
                  pl.BlockSpec((tk, tn), lambda i, j, k: (k, j)),
                  *extra_specs],
        out_specs=out_specs,
        out_shape=out_shape,
        scratch_shapes=[] if nk == 1 else [pltpu.VMEM((tm, tn), F32)],
        compiler_params=_params(("parallel", "parallel", "arbitrary")),
        name=name,
    )(x, w, *extra)


def _ep_store(acc, extra, outs, ij):
    outs[0][...] = acc.astype(outs[0].dtype)


def _ep_bias(acc, extra, outs, ij):
    outs[0][...] = acc + extra[0][...]


def _ep_sigmoid_from(acc, extra, outs, ij, *, first_gate_tile):
    @pl.when(ij[1] < first_gate_tile)
    def _():
        outs[0][...] = acc.astype(outs[0].dtype)

    @pl.when(ij[1] >= first_gate_tile)
    def _():
        outs[0][...] = jax.nn.sigmoid(acc).astype(outs[0].dtype)


def _mm_plain(x, w, tiles, out_dtype, name):
    m, n = x.shape[0], w.shape[1]
    tm, tn, tk = tiles
    tm, tn = min(tm, m), min(tn, n)
    return _matmul(x, w, tm=tm, tn=tn, tk=tk, epilogue=_ep_store,
                   out_shape=jax.ShapeDtypeStruct((m, n), out_dtype),
                   out_specs=pl.BlockSpec((tm, tn), lambda i, j, k: (i, j)), name=name)


def _mm_resid_body(x_ref, w_ref, r_ref, o_ref, *, alpha):
    @pl.when(pl.program_id(2) == 0)
    def _():
        o_ref[...] = alpha * r_ref[...]

    o_ref[...] += jnp.dot(x_ref[...], w_ref[...], preferred_element_type=F32)


def _mm_residual(x, w, resid, tiles, cfg, name):
    m, kdim = x.shape
    n = w.shape[1]
    tm, tn, tk = tiles
    tm, tn, tk = min(tm, m), min(tn, n), min(tk, kdim)
    assert m % tm == 0 and n % tn == 0 and kdim % tk == 0
    spec = pl.BlockSpec((tm, tn), lambda i, j, k: (i, j))
    return pl.pallas_call(
        functools.partial(_mm_resid_body, alpha=cfg.dn_alpha),
        grid=(m // tm, n // tn, kdim // tk),
        in_specs=[pl.BlockSpec((tm, tk), lambda i, j, k: (i, k)),
                  pl.BlockSpec((tk, tn), lambda i, j, k: (k, j)), spec],
        out_specs=spec,
        out_shape=jax.ShapeDtypeStruct((m, n), F32),
        compiler_params=_params(("parallel", "parallel", "arbitrary")),
        name=name,
    )(x, w, resid)


def _ln_body(y_ref, g_ref, b_ref, xf_ref, xb_ref, *, eps):
    y = y_ref[...]
    mu = jnp.mean(y, axis=-1, keepdims=True)
    d = y - mu
    var = jnp.mean(d * d, axis=-1, keepdims=True)
    out = d * lax.rsqrt(var + eps) * g_ref[...] + b_ref[...]
    xf_ref[...] = out
    xb_ref[...] = out.astype(BF16)


def _layer_norm(y, g, b, cfg):
    t, d = y.shape
    tr = min(cfg.t_row, t)
    row = pl.BlockSpec((tr, d), lambda i: (i, 0))
    vec = pl.BlockSpec((1, d), lambda i: (0, 0))
    return pl.pallas_call(
        functools.partial(_ln_body, eps=cfg.ln_eps),
        grid=(t // tr,),
        in_specs=[row, vec, vec],
        out_specs=[row, row],
        out_shape=[jax.ShapeDtypeStruct((t, d), F32), jax.ShapeDtypeStruct((t, d), BF16)],
        compiler_params=_params(("parallel",)),
        name="layer_norm",
    )(y, g.reshape(1, d), b.reshape(1, d))


def _split_specs(tr, d, na_tiles):
    return (pl.BlockSpec((tr, d), lambda i: (jnp.minimum(i, na_tiles - 1), 0)),
            pl.BlockSpec((tr, d), lambda i: (jnp.maximum(i - na_tiles, 0), 0)))


def _ln_split_body(y_ref, g_ref, b_ref, oa_ref, ob_ref, *, eps, na_tiles):
    i = pl.program_id(0)
    y = y_ref[...]
    mu = jnp.mean(y, axis=-1, keepdims=True)
    d = y - mu
    var = jnp.mean(d * d, axis=-1, keepdims=True)
    out = d * lax.rsqrt(var + eps) * g_ref[...] + b_ref[...]

    @pl.when(i < na_tiles)
    def _():
        oa_ref[...] = out

    @pl.when(i >= na_tiles)
    def _():
        ob_ref[...] = out


def _layer_norm_split(y, g, b, n_a, cfg):
    t, d = y.shape
    tr = min(cfg.t_row, t)
    assert n_a % tr == 0 and (t - n_a) % tr == 0 and 0 < n_a < t
    vec = pl.BlockSpec((1, d), lambda i: (0, 0))
    return pl.pallas_call(
        functools.partial(_ln_split_body, eps=cfg.ln_eps, na_tiles=n_a // tr),
        grid=(t // tr,),
        in_specs=[pl.BlockSpec((tr, d), lambda i: (i, 0)), vec, vec],
        out_specs=list(_split_specs(tr, d, n_a // tr)),
        out_shape=[jax.ShapeDtypeStruct((n_a, d), F32), jax.ShapeDtypeStruct((t - n_a, d), F32)],
        compiler_params=_params(("arbitrary",)),
        name="layer_norm_out",
    )(y, g.reshape(1, d), b.reshape(1, d))


def _stream_in_body(xa_ref, xb_ref, of_ref, ob_ref, *, na_tiles):
    i = pl.program_id(0)

    @pl.when(i < na_tiles)
    def _():
        of_ref[...] = xa_ref[...]
        ob_ref[...] = xa_ref[...].astype(BF16)

    @pl.when(i >= na_tiles)
    def _():
        of_ref[...] = xb_ref[...]
        ob_ref[...] = xb_ref[...].astype(BF16)


def _stream_in(xa, xb, cfg):
    (n_a, d), n_b = xa.shape, xb.shape[0]
    t = n_a + n_b
    tr = min(cfg.t_row, n_a, n_b)
    assert n_a % tr == 0 and n_b % tr == 0
    row = pl.BlockSpec((tr, d), lambda i: (i, 0))
    return pl.pallas_call(
        functools.partial(_stream_in_body, na_tiles=n_a // tr),
        grid=(t // tr,),
        in_specs=list(_split_specs(tr, d, n_a // tr)),
        out_specs=[row, row],
        out_shape=[jax.ShapeDtypeStruct((t, d), F32), jax.ShapeDtypeStruct((t, d), BF16)],
        compiler_params=_params(("arbitrary",)),
        name="stream_in",
    )(xa, xb)


def _mlstm_body(q_ref, k_ref, v_ref, g_ref, *rest, chunk, nc, reverse, reset_chunks, m_init, k_scale, combine_eps):
    if combine_eps is None:
        h_ref, c_scr, n_scr, m_scr = rest
    else:
        hf_ref, og_ref, ng_ref, h_ref, c_scr, n_scr, m_scr = rest
    c = pl.program_id(1)
    cc = nc - 1 - c if reverse else c

    @pl.when(functools.reduce(jnp.logical_or, [cc == r for r in reset_chunks]))
    def _():
        c_scr[...] = jnp.zeros_like(c_scr)
        n_scr[...] = jnp.zeros_like(n_scr)
        m_scr[...] = jnp.full_like(m_scr, m_init)

    gates = g_ref[0]
    ri, rf = (2, 3) if reverse else (0, 1)
    i_row = gates[ri:ri + 1, :]
    lf_row = jax.nn.log_sigmoid(gates[rf:rf + 1, :])

    row = lax.broadcasted_iota(jnp.int32, (chunk, chunk), 0)
    col = lax.broadcasted_iota(jnp.int32, (chunk, chunk), 1)
    visible = (col >= row) if reverse else (col <= row)
    visible_t = (row >= col) if reverse else (row <= col)
    b_col = jnp.sum(jnp.where(visible, lf_row, 0.0), axis=-1, keepdims=True)
    lf_col = jnp.sum(jnp.where(row == col, lf_row, 0.0), axis=-1, keepdims=True)
    b_row = jnp.sum(jnp.where(visible_t, lf_col, 0.0), axis=0, keepdims=True)

    m_prev = m_scr[0:1, 0:1]
    logd = jnp.where(visible, b_col - b_row + i_row, -jnp.inf)
    inter = b_col + m_prev
    m_row = jnp.maximum(jnp.max(logd, axis=-1, keepdims=True), inter)
    dmat = jnp.exp(logd - m_row)

    q = q_ref[...]
    k_f32 = k_ref[...].astype(F32) * k_scale
    k = k_f32.astype(BF16)
    v = v_ref[...]
    s = lax.dot_general(q, k, (((1,), (1,)), ((), ())), preferred_element_type=F32) * dmat
    w_inter = jnp.exp(inter - m_row)
    ct = c_scr[...]
    qc = jnp.dot(q, ct.astype(BF16), preferred_element_type=F32)
    num = jnp.dot(s.astype(BF16), v, preferred_element_type=F32) + w_inter * qc
    qn = jnp.dot(q, n_scr[...].astype(BF16), preferred_element_type=F32)[:, 0:1]
    den = jnp.sum(s, axis=-1, keepdims=True) + w_inter * qn
    h = num / jnp.maximum(jnp.abs(den), jnp.exp(-m_row))
    if combine_eps is None:
        h_ref[...] = h
    else:
        h = h + hf_ref[...]
        h = h * lax.rsqrt(jnp.mean(h * h, axis=-1, keepdims=True) + combine_eps)
        h_ref[...] = (h * ng_ref[...] * og_ref[...].astype(F32)).astype(h_ref.dtype)

    bl = jnp.sum(lf_row, axis=-1, keepdims=True)
    a_row = bl - b_row + i_row
    m_new = jnp.maximum(bl + m_prev, jnp.max(a_row, axis=-1, keepdims=True))
    decay = jnp.exp(bl + m_prev - m_new)
    wk_t = k_f32.T * jnp.exp(a_row - m_new)
    c_scr[...] = decay * ct + jnp.dot(wk_t.astype(BF16), v, preferred_element_type=F32)
    n_scr[...] = decay * n_scr[...] + jnp.sum(wk_t, axis=-1, keepdims=True)
    m_scr[...] = jnp.broadcast_to(m_new, m_scr.shape)


def _mlstm_scan(qkv, gates, seqs, cfg, *, reverse, h_other=None, norm_g=None):
    t = qkv.shape[0]
    nh, dqk, dv, chunk = cfg.ml_heads, cfg.ml_dqk, cfg.ml_dv, cfg.ml_chunk
    chunk = min(chunk, min(n for _, n in seqs))
    assert all(s % chunk == 0 and n % chunk == 0 for s, n in seqs)
    nc = t // chunk
    if reverse:
        resets = tuple((s + n) // chunk - 1 for s, n in seqs)
    else:
        resets = tuple(s // chunk for s, _ in seqs)

    def ci(c):
        return nc - 1 - c if reverse else c

    assert (2 * nh * dqk) % dv == 0
    v_off = 2 * nh * dqk // dv
    o_off = v_off + nh
    combine = h_other is not None
    h_spec = pl.BlockSpec((chunk, dv), lambda h, c: (ci(c), h))
    in_specs = [pl.BlockSpec((chunk, dqk), lambda h, c: (ci(c), h)),
                pl.BlockSpec((chunk, dqk), lambda h, c: (ci(c), nh + h)),
                pl.BlockSpec((chunk, dv), lambda h, c: (ci(c), v_off + h)),
                pl.BlockSpec((1, 8, chunk), lambda h, c: (h, 0, ci(c)))]
    operands = [qkv, qkv, qkv, gates]
    if combine:
        in_specs += [h_spec, pl.BlockSpec((chunk, dv), lambda h, c: (ci(c), o_off + h)),
                     pl.BlockSpec((1, dv), lambda h, c: (0, h))]
        operands += [h_other, qkv, norm_g.reshape(1, nh * dv)]
    return pl.pallas_call(
        functools.partial(_mlstm_body, chunk=chunk, nc=nc, reverse=reverse, reset_chunks=resets,
                          m_init=cfg.ml_m_init, k_scale=dqk ** -0.5,
                          combine_eps=cfg.rms_eps if combine else None),
        grid=(nh, nc),
        in_specs=in_specs,
        out_specs=h_spec,
        out_shape=jax.ShapeDtypeStruct((t, nh * dv), BF16 if combine else F32),
        scratch_shapes=[pltpu.VMEM((dqk, dv), F32), pltpu.VMEM((dqk, LANE), F32), pltpu.VMEM((8, LANE), F32)],
        compiler_params=_params(("parallel", "arbitrary")),
        name="mlstm_scan_bw" if reverse else "mlstm_scan_fw",
    )(*operands)


def _mlstm_mixer(xf, xb, w_in, b_gates, norm_g, w_out, seqs, cfg):
    t = xf.shape[0]
    nh, dqk, dv = cfg.ml_heads, cfg.ml_dqk, cfg.ml_dv
    qk_w, v_w = nh * dqk, nh * dv
    main_w = 2 * qk_w + 2 * v_w
    n_gates = 4 * nh
    w_main = w_in[:, :main_w].astype(BF16)
    w_gate = jnp.pad(w_in[:, main_w:], ((0, 0), (0, LANE - n_gates))).astype(BF16)
    b_gate = jnp.pad(b_gates, (0, LANE - n_gates)).reshape(1, LANE)

    tm, tn, tk = cfg.mm_wide
    tm, tn = min(tm, t), min(tn, v_w)
    assert (2 * qk_w + v_w) % tn == 0
    proj = _matmul(xb, w_main, tm=tm, tn=tn, tk=tk,
                   epilogue=functools.partial(_ep_sigmoid_from, first_gate_tile=(2 * qk_w + v_w) // tn),
                   out_shape=jax.ShapeDtypeStruct((t, main_w), BF16),
                   out_specs=pl.BlockSpec((tm, tn), lambda i, j, k: (i, j)), name="mlstm_in_proj")
    gates = _matmul(xb, w_gate, tm=tm, tn=LANE, tk=tk, epilogue=_ep_bias,
                    out_shape=jax.ShapeDtypeStruct((t, LANE), F32),
                    out_specs=pl.BlockSpec((tm, LANE), lambda i, j, k: (i, 0)),
                    extra=(b_gate,), extra_specs=(pl.BlockSpec((1, LANE), lambda i, j, k: (0, 0)),),
                    name="mlstm_gate_proj")
    g = gates[:, :n_gates].reshape(t, 4, nh).transpose(2, 1, 0)
    g = jnp.pad(g, ((0, 0), (0, 4), (0, 0)))
    h_fw = _mlstm_scan(proj, g, seqs, cfg, reverse=False)
    hact = _mlstm_scan(proj, g, seqs, cfg, reverse=True, h_other=h_fw, norm_g=norm_g)
    return _mm_residual(hact, w_out.astype(BF16), xf, cfg.mm_wide, cfg, "mlstm_out_proj")


def _rope_tables(max_len, cfg, scale):
    half = cfg.rope // 2
    inv_freq = jnp.power(cfg.rope_base, -jnp.arange(0, cfg.rope, 2, dtype=F32) / cfg.rope)
    ang = jnp.arange(max_len, dtype=F32)[:, None] * inv_freq[None, :]
    cos, sin = jnp.cos(ang), jnp.sin(ang)
    z = jnp.zeros((max_len, LANE - 2 * half), F32)
    zh = jnp.zeros((max_len, half), F32)
    cos_t = jnp.concatenate([cos, cos, z], axis=1) * scale
    sin_a = jnp.concatenate([-sin, zh, z], axis=1) * scale
    sin_b = jnp.concatenate([zh, sin, z], axis=1) * scale
    return cos_t, sin_a, sin_b


def _rotate(x, cos_t, sin_a, sin_b, half):
    return x * cos_t + pltpu.roll(x, LANE - half, 1) * sin_a + pltpu.roll(x, half, 1) * sin_b


def _ep_mla_in(acc, extra, outs, ij, *, q_rank, kv_rank, eps):
    gq_ref, gkv_ref = extra
    cq_ref, ckv_ref, kr_ref = outs

    def rms(x, g):
        return x * lax.rsqrt(jnp.mean(x * x, axis=-1, keepdims=True) + eps) * g

    cq_ref[...] = rms(acc[:, :q_rank], gq_ref[...]).astype(cq_ref.dtype)
    ckv_ref[...] = rms(acc[:, q_rank:q_rank + kv_rank], gkv_ref[...]).astype(ckv_ref.dtype)
    kr_ref[...] = acc[:, q_rank + kv_rank:]


def _ep_q_up(acc, extra, outs, ij, *, heads, scale, half):
    cos_ref, sa_ref, sb_ref = extra
    out = outs[0]
    for hh in range(heads):
        lo = hh * 2 * LANE
        out[:, lo:lo + LANE] = (acc[:, lo:lo + LANE] * scale).astype(out.dtype)
        r = acc[:, lo + LANE:lo + 2 * LANE]
        out[:, lo + LANE:lo + 2 * LANE] = _rotate(r, cos_ref[...], sa_ref[...], sb_ref[...], half).astype(out.dtype)


def _kv_up_body(x_ref, wk_ref, wv_ref, kr_ref, cos_ref, sa_ref, sb_ref, k_out, v_out, *, heads, half):
    x = x_ref[...]
    kk = jnp.dot(x, wk_ref[...], preferred_element_type=F32)
    vv = jnp.dot(x, wv_ref[...], preferred_element_type=F32)
    kr = _rotate(kr_ref[...], cos_ref[...], sa_ref[...], sb_ref[...], half).astype(k_out.dtype)
    ones = jnp.ones((x.shape[0], LANE), v_out.dtype)
    for hh in range(heads):
        lo = hh * 2 * LANE
        k_out[:, lo:lo + LANE] = kk[:, hh * LANE:(hh + 1) * LANE].astype(k_out.dtype)
        k_out[:, lo + LANE:lo + 2 * LANE] = kr
        v_out[:, lo:lo + LANE] = vv[:, hh * LANE:(hh + 1) * LANE].astype(v_out.dtype)
        v_out[:, lo + LANE:lo + 2 * LANE] = ones


def _pos_block(seqs, tm):
    def f(i):
        out = i
        for s, _ in seqs:
            out = jnp.where(i >= s // tm, i - s // tm, out)
        return out
    return f


def _attn_body(q_ref, k_ref, v_ref, o_ref, m_scr, acc_scr, *, nkv, heads, qw, vd):
    ki = pl.program_id(3)

    @pl.when(ki == 0)
    def _():
        m_scr[...] = jnp.full_like(m_scr, -jnp.inf)
        acc_scr[...] = jnp.zeros_like(acc_scr)

    tkv = k_ref.shape[0]
    for hh in range(heads):
        q = q_ref[:, hh * qw:(hh + 1) * qw]
        k = k_ref[:, hh * qw:(hh + 1) * qw]
        v = v_ref[:, hh * 2 * vd:(hh + 1) * 2 * vd]
        s = lax.dot_general(q, k, (((1,), (1,)), ((), ())), preferred_element_type=F32)
        m_prev = m_scr[hh]
        m_new = jnp.maximum(m_prev, jnp.broadcast_to(jnp.max(s, axis=-1, keepdims=True), m_prev.shape))
        alpha = jnp.exp2(m_prev - m_new)
        p = jnp.exp2(s - jnp.concatenate([m_new] * (tkv // LANE), axis=1))
        pv = jnp.dot(p.astype(v.dtype), v, preferred_element_type=F32)
        acc_scr[hh] = jnp.concatenate([alpha] * (2 * vd // LANE), axis=1) * acc_scr[hh] + pv
        m_scr[hh] = m_new

    @pl.when(ki == nkv - 1)
    def _():
        for hh in range(heads):
            o_ref[:, hh * vd:(hh + 1) * vd] = (acc_scr[hh, :, :vd] / acc_scr[hh, :, vd:]).astype(o_ref.dtype)


def _attention(q, k, v, row0, nseq, slen, cfg):
    nh, vd, qw = cfg.mla_heads, cfg.vdim, cfg.qk_pad
    assert vd == LANE
    hp = min(cfg.attn_heads_per_step, nh)
    tq, tkv = min(cfg.tq, slen), min(cfg.tkv, slen)
    assert slen % tq == 0 and slen % tkv == 0 and row0 % tq == 0 and row0 % tkv == 0 and nh % hp == 0
    nq, nkv = slen // tq, slen // tkv
    q0, k0 = row0 // tq, row0 // tkv
    return pl.pallas_call(
        functools.partial(_attn_body, nkv=nkv, heads=hp, qw=qw, vd=vd),
        grid=(nseq, nh // hp, nq, nkv),
        in_specs=[pl.BlockSpec((tq, hp * qw), lambda b, h, qi, ki: (q0 + b * nq + qi, h)),
                  pl.BlockSpec((tkv, hp * qw), lambda b, h, qi, ki: (k0 + b * nkv + ki, h)),
                  pl.BlockSpec((tkv, hp * 2 * vd), lambda b, h, qi, ki: (k0 + b * nkv + ki, h))],
        out_specs=pl.BlockSpec((tq, hp * vd), lambda b, h, qi, ki: (b * nq + qi, h)),
        out_shape=jax.ShapeDtypeStruct((nseq * slen, nh * vd), BF16),
        scratch_shapes=[pltpu.VMEM((hp, tq, LANE), F32), pltpu.VMEM((hp, tq, 2 * vd), F32)],
        compiler_params=_params(("parallel", "parallel", "parallel", "arbitrary")),
        name="mla_attention",
    )(q, k, v)


def _mla_mixer(xf, xb, w_in, q_norm_g, kv_norm_g, w_uq, w_ukv, w_out, seqs, cfg):
    t, d = xf.shape
    nh, nope, rope, vd = cfg.mla_heads, cfg.nope, cfg.rope, cfg.vdim
    qr, kvr = cfg.q_rank, cfg.kv_rank
    half = rope // 2
    assert nope == LANE and vd == LANE and rope <= LANE
    qw = cfg.qk_pad
    tm = min(cfg.mm_q_up[0], t)
    tm_in = min(cfg.mm_mla_in[0], t)
    max_len = max(n for _, n in seqs)
    pos = _pos_block(seqs, tm)

    w_in_p = jnp.pad(w_in, ((0, 0), (0, LANE - rope))).astype(BF16)
    w_uq_p = jnp.pad(w_uq.reshape(qr, nh, nope + rope), ((0, 0), (0, 0), (0, qw - nope - rope)))
    w_uq_p = w_uq_p.reshape(qr, nh * qw).astype(BF16)
    w_ukv_h = w_ukv.reshape(kvr, nh, nope + vd)
    w_uk = w_ukv_h[:, :, :nope].reshape(kvr, nh * nope).astype(BF16)
    w_uv = w_ukv_h[:, :, nope:].reshape(kvr, nh * vd).astype(BF16)

    n_in = qr + kvr + LANE
    cq, ckv, kr = _matmul(
        xb, w_in_p, tm=tm_in, tn=n_in, tk=cfg.mm_mla_in[2],
        epilogue=functools.partial(_ep_mla_in, q_rank=qr, kv_rank=kvr, eps=cfg.rms_eps),
        out_shape=[jax.ShapeDtypeStruct((t, qr), BF16), jax.ShapeDtypeStruct((t, kvr), BF16),
                   jax.ShapeDtypeStruct((t, LANE), F32)],
        out_specs=[pl.BlockSpec((tm_in, qr), lambda i, j, k: (i, 0)),
                   pl.BlockSpec((tm_in, kvr), lambda i, j, k: (i, 0)),
                   pl.BlockSpec((tm_in, LANE), lambda i, j, k: (i, 0))],
        extra=(q_norm_g.reshape(1, qr), kv_norm_g.reshape(1, kvr)),
        extra_specs=(pl.BlockSpec((1, qr), lambda i, j, k: (0, 0)),
                     pl.BlockSpec((1, kvr), lambda i, j, k: (0, 0))),
        name="mla_in_proj")

    scale = (nope + rope) ** -0.5 * LOG2_E
    q_tabs = _rope_tables(max_len, cfg, scale)
    k_tabs = _rope_tables(max_len, cfg, 1.0)
    tn = min(cfg.mm_q_up[1], nh * qw)
    tab3 = pl.BlockSpec((tm, LANE), lambda i, j, k: (pos(i), 0))
    q = _matmul(cq, w_uq_p, tm=tm, tn=tn, tk=cfg.mm_q_up[2],
                epilogue=functools.partial(_ep_q_up, heads=tn // qw, scale=scale, half=half),
                out_shape=jax.ShapeDtypeStruct((t, nh * qw), BF16),
                out_specs=pl.BlockSpec((tm, tn), lambda i, j, k: (i, j)),
                extra=q_tabs, extra_specs=(tab3, tab3, tab3), name="mla_q_up")

    hb = min(cfg.kv_heads_per_step, nh)
    tab2 = pl.BlockSpec((tm, LANE), lambda i, j: (pos(i), 0))
    k, v = pl.pallas_call(
        functools.partial(_kv_up_body, heads=hb, half=half),
        grid=(t // tm, nh // hb),
        in_specs=[pl.BlockSpec((tm, kvr), lambda i, j: (i, 0)),
                  pl.BlockSpec((kvr, hb * nope), lambda i, j: (0, j)),
                  pl.BlockSpec((kvr, hb * vd), lambda i, j: (0, j)),
                  pl.BlockSpec((tm, LANE), lambda i, j: (i, 0)),
                  tab2, tab2, tab2],
        out_specs=[pl.BlockSpec((tm, hb * qw), lambda i, j: (i, j)),
                   pl.BlockSpec((tm, hb * 2 * vd), lambda i, j: (i, j))],
        out_shape=[jax.ShapeDtypeStruct((t, nh * qw), BF16), jax.ShapeDtypeStruct((t, nh * 2 * vd), BF16)],
        compiler_params=_params(("parallel", "parallel")),
        name="mla_kv_up",
    )(ckv, w_uk, w_uv, kr, *k_tabs)

    outs = []
    idx = 0
    while idx < len(seqs):
        row0, slen = seqs[idx]
        nseq = 1
        while idx + nseq < len(seqs) and seqs[idx + nseq] == (row0 + nseq * slen, slen):
            nseq += 1
        outs.append(_attention(q, k, v, row0, nseq, slen, cfg))
        idx += nseq
    o = outs[0] if len(outs) == 1 else jnp.concatenate(outs, axis=0)
    return _mm_residual(o, w_out.astype(BF16), xf, cfg.mm_wide, cfg, "mla_out_proj")


def _ffn_up_body(x_ref, wg_ref, wu_ref, hg_ref, hu_ref, cwg_ref, cwu_ref, cbg_ref, cbu_ref, out_ref, *,
                 tm, seq_starts, seq_ends):
    r0 = pl.program_id(0) * tm
    at_start = functools.reduce(jnp.logical_or, [r0 == s for s in seq_starts])
    at_end = functools.reduce(jnp.logical_or, [r0 + tm == e for e in seq_ends])
    x = x_ref[...]

    def conv(w_ref, halo_ref, cw_ref, cb_ref):
        h = jnp.dot(x, w_ref[...], preferred_element_type=F32)
        prev_row = jnp.where(at_start, 0.0, halo_ref[0:1, :])
        next_row = jnp.where(at_end, 0.0, halo_ref[1:2, :])
        h_prev = jnp.concatenate([prev_row, h[:tm - 1, :]], axis=0)
        h_next = jnp.concatenate([h[1:, :], next_row], axis=0)
        cw = cw_ref[...]
        return h_prev * cw[0:1, :] + h * cw[1:2, :] + h_next * cw[2:3, :] + cb_ref[...]

    g = conv(wg_ref, hg_ref, cwg_ref, cbg_ref)
    u = conv(wu_ref, hu_ref, cwu_ref, cbu_ref)
    out_ref[...] = (g * jax.nn.sigmoid(g) * u).astype(out_ref.dtype)


def _conv_ffn(xf, xb, w_up, conv_w, conv_b, w_down, seqs, cfg):
    t, d = xb.shape
    f, fp, cbw = cfg.d_ff, cfg.d_ff_pad, cfg.ff_block
    pad = fp - f
    nb = fp // cbw
    assert fp % cbw == 0

    def pad_halves(a):
        lead = [(0, 0)] * (a.ndim - 1)
        return jnp.concatenate([jnp.pad(a[..., :f], lead + [(0, pad)]), jnp.pad(a[..., f:], lead + [(0, pad)])], -1)

    w_up_p = pad_halves(w_up.astype(BF16))
    w_down_p = jnp.pad(w_down.astype(BF16), ((0, pad), (0, 0)))
    cw, cb = pad_halves(conv_w), pad_halves(conv_b.reshape(1, -1))

    tm, _, tk = cfg.mm_wide
    tm = min(tm, t)
    nt = t // tm
    assert tk == d
    halo_idx = np.array([[max(i * tm - 1, 0), min((i + 1) * tm, t - 1)] * 4 for i in range(nt)], np.int32).reshape(-1)
    h_halo = _mm_plain(xb[halo_idx], w_up_p, (nt * 8, cfg.mm_halo_tn, tk), F32, "ffn_up_halo")

    def gate_cols(rows):
        return pl.BlockSpec((rows, cbw), lambda i, j: (0, j))

    def up_cols(rows):
        return pl.BlockSpec((rows, cbw), lambda i, j: (0, nb + j))

    act = pl.pallas_call(
        functools.partial(_ffn_up_body, tm=tm,
                          seq_starts=tuple(s for s, _ in seqs), seq_ends=tuple(s + n for s, n in seqs)),
        grid=(nt, nb),
        in_specs=[pl.BlockSpec((tm, d), lambda i, j: (i, 0)),
                  gate_cols(d), up_cols(d),
                  pl.BlockSpec((8, cbw), lambda i, j: (i, j)), pl.BlockSpec((8, cbw), lambda i, j: (i, nb + j)),
                  gate_cols(3), up_cols(3), gate_cols(1), up_cols(1)],
        out_specs=pl.BlockSpec((tm, cbw), lambda i, j: (i, j)),
        out_shape=jax.ShapeDtypeStruct((t, fp), BF16),
        compiler_params=_params(("parallel", "parallel")),
        name="ffn_up_conv_gate",
    )(xb, w_up_p, w_up_p, h_halo, h_halo, cw, cw, cb, cb)
    return _mm_residual(act, w_down_p, xf, cfg.mm_down, cfg, "ffn_down")


def _trunk(xf, xb, n_first, seqs, cfg, ml_w_in, ml_b_gates, ml_norm_g, ml_w_out, mla_w_in, mla_q_norm_g,
           mla_kv_norm_g, mla_w_uq, mla_w_ukv, mla_w_out, ffn_w_up, ffn_conv_w, ffn_conv_b, ffn_w_down,
           ln1_g, ln1_b, ln2_g, ln2_b):
    for i in range(cfg.depth):
        j = i // 2
        if i % 2 == 0:
            y = _mlstm_mixer(xf, xb, ml_w_in[j], ml_b_gates[j], ml_norm_g[j], ml_w_out[j], seqs, cfg)
        else:
            y = _mla_mixer(xf, xb, mla_w_in[j], mla_q_norm_g[j], mla_kv_norm_g[j],
                           mla_w_uq[j], mla_w_ukv[j], mla_w_out[j], seqs, cfg)
        xf, xb = _layer_norm(y, ln1_g[i], ln1_b[i], cfg)
        y = _conv_ffn(xf, xb, ffn_w_up[i], ffn_conv_w[i], ffn_conv_b[i], ffn_w_down[i], seqs, cfg)
        if i == cfg.depth - 1:
            return _layer_norm_split(y, ln2_g[i], ln2_b[i], n_first, cfg)
        xf, xb = _layer_norm(y, ln2_g[i], ln2_b[i], cfg)


def _run(cfg, x_prompt, x_sample, *weights):
    d = x_prompt.shape[-1]
    shapes = [x_prompt.shape, x_sample.shape]
    seqs, row = [], 0
    for b, s, _ in shapes:
        for _ in range(b):
            seqs.append((row, s))
            row += s
    xf, xb = _stream_in(x_prompt.reshape(-1, d), x_sample.reshape(-1, d), cfg)
    n_prompt = shapes[0][0] * shapes[0][1]
    y_prompt, y_sample = _trunk(xf, xb, n_prompt, tuple(seqs), cfg, *weights)
    return y_prompt.reshape(shapes[0]), y_sample.reshape(shapes[1])


def kernel(x_prompt, x_sample, ml_w_in, ml_b_gates, ml_norm_g, ml_w_out, mla_w_in, mla_q_norm_g, mla_kv_norm_g, mla_w_uq, mla_w_ukv, mla_w_out, ffn_w_up, ffn_conv_w, ffn_conv_b, ffn_w_down, ln1_g, ln1_b, ln2_g, ln2_b):
    return _run(Cfg(), x_prompt, x_sample, ml_w_in, ml_b_gates, ml_norm_g, ml_w_out, mla_w_in, mla_q_norm_g,
                mla_kv_norm_g, mla_w_uq, mla_w_ukv, mla_w_out, ffn_w_up, ffn_conv_w, ffn_conv_b, ffn_w_down,
                ln1_g, ln1_b, ln2_g, ln2_b)
```

```python
import functools
from typing import NamedTuple

import jax
import jax.numpy as jnp
import numpy as np
from jax import lax
from jax.experimental import pallas as pl
from jax.experimental.pallas import tpu as pltpu

F32 = jnp.float32
BF16 = jnp.bfloat16

LANE = 128
LOG2_E = 1.4426950408889634
VMEM_LIMIT = 56 * 1024 * 1024


class Cfg(NamedTuple):
    d_model: int = 4096
    depth: int = 4
    ml_heads: int = 8
    ml_dqk: int = 256
    ml_dv: int = 512
    ml_chunk: int = 512
    ml_m_init: float = -1e30
    mla_heads: int = 32
    nope: int = 128
    rope: int = 64
    vdim: int = 128
    q_rank: int = 1024
    kv_rank: int = 512
    rope_base: float = 10000.0
    d_ff: int = 11008
    d_ff_pad: int = 11264
    ln_eps: float = 1e-5
    rms_eps: float = 1e-6
    mm_wide: tuple = (1024, 1024, 4096)
    mm_down: tuple = (1024, 1024, 2816)
    mm_mla_in: tuple = (512, 0, 4096)
    mm_q_up: tuple = (1024, 1024, 1024)
    mm_halo_tn: int = 1024
    ff_block: int = 512
    t_row: int = 256
    tq: int = 1024
    tkv: int = 1024
    attn_heads_per_step: int = 8
    kv_heads_per_step: int = 4

    @property
    def dn_alpha(self):
        return (2 * self.depth) ** 0.25

    @property
    def qk_pad(self):
        return 2 * LANE


def _params(sem, vmem=VMEM_LIMIT):
    return pltpu.CompilerParams(dimension_semantics=sem, vmem_limit_bytes=vmem)


def _mm_body(*refs, nk, n_extra, epilogue):
    x_ref, w_ref = refs[0], refs[1]
    extra = refs[2:2 + n_extra]
    ij, k = (pl.program_id(0), pl.program_id(1)), pl.program_id(2)
    part = jnp.dot(x_ref[...], w_ref[...], preferred_element_type=F32)
    if nk == 1:
        epilogue(part, extra, refs[2 + n_extra:], ij)
        return
    outs, acc_ref = refs[2 + n_extra:-1], refs[-1]

    @pl.when(k == 0)
    def _():
        acc_ref[...] = part

    @pl.when(jnp.logical_and(k > 0, k < nk - 1))
    def _():
        acc_ref[...] += part

    @pl.when(k == nk - 1)
    def _():
        epilogue(acc_ref[...] + part, extra, outs, ij)


def _matmul(x, w, *, tm, tn, tk, epilogue, out_shape, out_specs, extra=(), extra_specs=(), name):
    m, kdim = x.shape
    n = w.shape[1]
    tm, tn, tk = min(tm, m), min(tn, n), min(tk, kdim)
    assert m % tm == 0 and n % tn == 0 and kdim % tk == 0, (x.shape, w.shape, tm, tn, tk)
    nk = kdim // tk
    return pl.pallas_call(
        functools.partial(_mm_body, nk=nk, n_extra=len(extra), epilogue=epilogue),
        grid=(m // tm, n // tn, nk),
        in_specs=[pl.BlockSpec((tm, tk), lambda i, j, k: (i, k)),
                  pl.BlockSpec((tk, tn), lambda i, j, k: (k, j)),
                  *extra_specs],
        out_specs=out_specs,
        out_shape=out_shape,
        scratch_shapes=[] if nk == 1 else [pltpu.VMEM((tm, tn), F32)],
        compiler_params=_params(("parallel", "parallel", "arbitrary")),
        name=name,
    )(x, w, *extra)


def _ep_store(acc, extra, outs, ij):
    outs[0][...] = acc.astype(outs[0].dtype)


def _ep_bias(acc, extra, outs, ij):
    outs[0][...] = acc + extra[0][...]


def _ep_sigmoid(acc, extra, outs, ij):
    outs[0][...] = jax.nn.sigmoid(acc).astype(outs[0].dtype)


def _mm_plain(x, w, tiles, out_dtype, name):
    m, n = x.shape[0], w.shape[1]
    tm, tn, tk = tiles
    tm, tn = min(tm, m), min(tn, n)
    return _matmul(x, w, tm=tm, tn=tn, tk=tk, epilogue=_ep_store,
                   out_shape=jax.ShapeDtypeStruct((m, n), out_dtype),
                   out_specs=pl.BlockSpec((tm, tn), lambda i, j, k: (i, j)), name=name)


def _mm_resid_body(x_ref, w_ref, r_ref, o_ref, *, alpha):
    @pl.when(pl.program_id(2) == 0)
    def _():
        o_ref[...] = alpha * r_ref[...]

    o_ref[...] += jnp.dot(x_ref[...], w_ref[...], preferred_element_type=F32)


def _mm_residual(x, w, resid, tiles, cfg, name):
    m, kdim = x.shape
    n = w.shape[1]
    tm, tn, tk = tiles
    tm, tn, tk = min(tm, m), min(tn, n), min(tk, kdim)
    assert m % tm == 0 and n % tn == 0 and kdim % tk == 0
    spec = pl.BlockSpec((tm, tn), lambda i, j, k: (i, j))
    return pl.pallas_call(
        functools.partial(_mm_resid_body, alpha=cfg.dn_alpha),
        grid=(m // tm, n // tn, kdim // tk),
        in_specs=[pl.BlockSpec((tm, tk), lambda i, j, k: (i, k)),
                  pl.BlockSpec((tk, tn), lambda i, j, k: (k, j)), spec],
        out_specs=spec,
        out_shape=jax.ShapeDtypeStruct((m, n), F32),
        compiler_params=_params(("parallel", "parallel", "arbitrary")),
        name=name,
    )(x, w, resid)


def _ln_body(y_ref, g_ref, b_ref, xf_ref, xb_ref, *, eps):
    y = y_ref[...]
    mu = jnp.mean(y, axis=-1, keepdims=True)
    d = y - mu
    var = jnp.mean(d * d, axis=-1, keepdims=True)
    out = d * lax.rsqrt(var + eps) * g_ref[...] + b_ref[...]
    xf_ref[...] = out
    xb_ref[...] = out.astype(BF16)


def _layer_norm(y, g, b, cfg):
    t, d = y.shape
    tr = min(cfg.t_row, t)
    row = pl.BlockSpec((tr, d), lambda i: (i, 0))
    vec = pl.BlockSpec((1, d), lambda i: (0, 0))
    return pl.pallas_call(
        functools.partial(_ln_body, eps=cfg.ln_eps),
        grid=(t // tr,),
        in_specs=[row, vec, vec],
        out_specs=[row, row],
        out_shape=[jax.ShapeDtypeStruct((t, d), F32), jax.ShapeDtypeStruct((t, d), BF16)],
        compiler_params=_params(("parallel",)),
        name="layer_norm",
    )(y, g.reshape(1, d), b.reshape(1, d))


def _split_specs(tr, d, na_tiles):
    return (pl.BlockSpec((tr, d), lambda i: (jnp.minimum(i, na_tiles - 1), 0)),
            pl.BlockSpec((tr, d), lambda i: (jnp.maximum(i - na_tiles, 0), 0)))


def _ln_split_body(y_ref, g_ref, b_ref, oa_ref, ob_ref, *, eps, na_tiles):
    i = pl.program_id(0)
    y = y_ref[...]
    mu = jnp.mean(y, axis=-1, keepdims=True)
    d = y - mu
    var = jnp.mean(d * d, axis=-1, keepdims=True)
    out = d * lax.rsqrt(var + eps) * g_ref[...] + b_ref[...]

    @pl.when(i < na_tiles)
    def _():
        oa_ref[...] = out

    @pl.when(i >= na_tiles)
    def _():
        ob_ref[...] = out


def _layer_norm_split(y, g, b, n_a, cfg):
    t, d = y.shape
    tr = min(cfg.t_row, t)
    assert n_a % tr == 0 and (t - n_a) % tr == 0 and 0 < n_a < t
    vec = pl.BlockSpec((1, d), lambda i: (0, 0))
    return pl.pallas_call(
        functools.partial(_ln_split_body, eps=cfg.ln_eps, na_tiles=n_a // tr),
        grid=(t // tr,),
        in_specs=[pl.BlockSpec((tr, d), lambda i: (i, 0)), vec, vec],
        out_specs=list(_split_specs(tr, d, n_a // tr)),
        out_shape=[jax.ShapeDtypeStruct((n_a, d), F32), jax.ShapeDtypeStruct((t - n_a, d), F32)],
        compiler_params=_params(("arbitrary",)),
        name="layer_norm_out",
    )(y, g.reshape(1, d), b.reshape(1, d))


def _stream_in_body(xa_ref, xb_ref, of_ref, ob_ref, *, na_tiles):
    i = pl.program_id(0)

    @pl.when(i < na_tiles)
    def _():
        of_ref[...] = xa_ref[...]
        ob_ref[...] = xa_ref[...].astype(BF16)

    @pl.when(i >= na_tiles)
    def _():
        of_ref[...] = xb_ref[...]
        ob_ref[...] = xb_ref[...].astype(BF16)


def _stream_in(xa, xb, cfg):
    (n_a, d), n_b = xa.shape, xb.shape[0]
    t = n_a + n_b
    tr = min(cfg.t_row, n_a, n_b)
    assert n_a % tr == 0 and n_b % tr == 0
    row = pl.BlockSpec((tr, d), lambda i: (i, 0))
    return pl.pallas_call(
        functools.partial(_stream_in_body, na_tiles=n_a // tr),
        grid=(t // tr,),
        in_specs=list(_split_specs(tr, d, n_a // tr)),
        out_specs=[row, row],
        out_shape=[jax.ShapeDtypeStruct((t, d), F32), jax.ShapeDtypeStruct((t, d), BF16)],
        compiler_params=_params(("arbitrary",)),
        name="stream_in",
    )(xa, xb)


def _mlstm_body(q_ref, k_ref, v_ref, g_ref, *rest, chunk, nc, reverse, reset_chunks, m_init, k_scale, combine_eps):
    if combine_eps is None:
        h_ref, c_scr, n_scr, m_scr = rest
    else:
        hf_ref, og_ref, ng_ref, h_ref, c_scr, n_scr, m_scr = rest
    c = pl.program_id(1)
    cc = nc - 1 - c if reverse else c

    @pl.when(functools.reduce(jnp.logical_or, [cc == r for r in reset_chunks]))
    def _():
        c_scr[...] = jnp.zeros_like(c_scr)
        n_scr[...] = jnp.zeros_like(n_scr)
        m_scr[...] = jnp.full_like(m_scr, m_init)

    gates = g_ref[0]
    ri, rf = (2, 3) if reverse else (0, 1)
    i_row = gates[ri:ri + 1, :]
    lf_row = jax.nn.log_sigmoid(gates[rf:rf + 1, :])

    row = lax.broadcasted_iota(jnp.int32, (chunk, chunk), 0)
    col = lax.broadcasted_iota(jnp.int32, (chunk, chunk), 1)
    visible = (col >= row) if reverse else (col <= row)
    visible_t = (row >= col) if reverse else (row <= col)
    b_col = jnp.sum(jnp.where(visible, lf_row, 0.0), axis=-1, keepdims=True)
    lf_col = jnp.sum(jnp.where(row == col, lf_row, 0.0), axis=-1, keepdims=True)
    b_row = jnp.sum(jnp.where(visible_t, lf_col, 0.0), axis=0, keepdims=True)

    m_prev = m_scr[0:1, 0:1]
    logd = jnp.where(visible, b_col - b_row + i_row, -jnp.inf)
    inter = b_col + m_prev
    m_row = jnp.maximum(jnp.max(logd, axis=-1, keepdims=True), inter)
    dmat = jnp.exp(logd - m_row)

    q = q_ref[...]
    k_f32 = k_ref[...].astype(F32) * k_scale
    k = k_f32.astype(BF16)
    v = v_ref[...]
    s = lax.dot_general(q, k, (((1,), (1,)), ((), ())), preferred_element_type=F32) * dmat
    w_inter = jnp.exp(inter - m_row)
    ct = c_scr[...]
    qc = jnp.dot(q, ct.astype(BF16), preferred_element_type=F32)
    num = jnp.dot(s.astype(BF16), v, preferred_element_type=F32) + w_inter * qc
    qn = jnp.dot(q, n_scr[...].astype(BF16), preferred_element_type=F32)[:, 0:1]
    den = jnp.sum(s, axis=-1, keepdims=True) + w_inter * qn
    h = num / jnp.maximum(jnp.abs(den), jnp.exp(-m_row))
    if combine_eps is None:
        h_ref[...] = h
    else:
        h = h + hf_ref[...]
        h = h * lax.rsqrt(jnp.mean(h * h, axis=-1, keepdims=True) + combine_eps)
        h_ref[...] = (h * ng_ref[...] * og_ref[...].astype(F32)).astype(h_ref.dtype)

    bl = jnp.sum(lf_row, axis=-1, keepdims=True)
    a_row = bl - b_row + i_row
    m_new = jnp.maximum(bl + m_prev, jnp.max(a_row, axis=-1, keepdims=True))
    decay = jnp.exp(bl + m_prev - m_new)
    wk_t = k_f32.T * jnp.exp(a_row - m_new)
    c_scr[...] = decay * ct + jnp.dot(wk_t.astype(BF16), v, preferred_element_type=F32)
    n_scr[...] = decay * n_scr[...] + jnp.sum(wk_t, axis=-1, keepdims=True)
    m_scr[...] = jnp.broadcast_to(m_new, m_scr.shape)


def _mlstm_scan(qkv, gates, seqs, cfg, *, reverse, h_other=None, out_gate=None, norm_g=None):
    t = qkv.shape[0]
    nh, dqk, dv, chunk = cfg.ml_heads, cfg.ml_dqk, cfg.ml_dv, cfg.ml_chunk
    chunk = min(chunk, min(n for _, n in seqs))
    assert all(s % chunk == 0 and n % chunk == 0 for s, n in seqs)
    nc = t // chunk
    if reverse:
        resets = tuple((s + n) // chunk - 1 for s, n in seqs)
    else:
        resets = tuple(s // chunk for s, _ in seqs)

    def ci(c):
        return nc - 1 - c if reverse else c

    assert (2 * nh * dqk) % dv == 0
    v_off = 2 * nh * dqk // dv
    combine = h_other is not None
    h_spec = pl.BlockSpec((chunk, dv), lambda h, c: (ci(c), h))
    in_specs = [pl.BlockSpec((chunk, dqk), lambda h, c: (ci(c), h)),
                pl.BlockSpec((chunk, dqk), lambda h, c: (ci(c), nh + h)),
                pl.BlockSpec((chunk, dv), lambda h, c: (ci(c), v_off + h)),
                pl.BlockSpec((1, 8, chunk), lambda h, c: (h, 0, ci(c)))]
    operands = [qkv, qkv, qkv, gates]
    if combine:
        in_specs += [h_spec, h_spec, pl.BlockSpec((1, dv), lambda h, c: (0, h))]
        operands += [h_other, out_gate, norm_g.reshape(1, nh * dv)]
    return pl.pallas_call(
        functools.partial(_mlstm_body, chunk=chunk, nc=nc, reverse=reverse, reset_chunks=resets,
                          m_init=cfg.ml_m_init, k_scale=dqk ** -0.5,
                          combine_eps=cfg.rms_eps if combine else None),
        grid=(nh, nc),
        in_specs=in_specs,
        out_specs=h_spec,
        out_shape=jax.ShapeDtypeStruct((t, nh * dv), BF16 if combine else F32),
        scratch_shapes=[pltpu.VMEM((dqk, dv), F32), pltpu.VMEM((dqk, LANE), F32), pltpu.VMEM((8, LANE), F32)],
        compiler_params=_params(("parallel", "arbitrary")),
        name="mlstm_scan_bw" if reverse else "mlstm_scan_fw",
    )(*operands)


def _mlstm_mixer(xf, xb, w_in, b_gates, norm_g, w_out, seqs, cfg):
    t = xf.shape[0]
    nh, dqk, dv = cfg.ml_heads, cfg.ml_dqk, cfg.ml_dv
    qk_w, v_w = nh * dqk, nh * dv
    qkv_w = 2 * qk_w + v_w
    main_w = qkv_w + v_w
    n_gates = 4 * nh
    w_qkv = w_in[:, :qkv_w].astype(BF16)
    w_o = w_in[:, qkv_w:main_w].astype(BF16)
    w_gate = jnp.pad(w_in[:, main_w:], ((0, 0), (0, LANE - n_gates))).astype(BF16)
    b_gate = jnp.pad(b_gates, (0, LANE - n_gates)).reshape(1, LANE)

    tm, tn, tk = cfg.mm_wide
    tm, tn = min(tm, t), min(tn, v_w)
    qkv = _mm_plain(xb, w_qkv, cfg.mm_wide, BF16, "mlstm_in_proj")
    out_gate = _matmul(xb, w_o, tm=tm, tn=tn, tk=tk, epilogue=_ep_sigmoid,
                       out_shape=jax.ShapeDtypeStruct((t, v_w), BF16),
                       out_specs=pl.BlockSpec((tm, tn), lambda i, j, k: (i, j)), name="mlstm_out_gate_proj")
    gates = _matmul(xb, w_gate, tm=tm, tn=LANE, tk=tk, epilogue=_ep_bias,
                    out_shape=jax.ShapeDtypeStruct((t, LANE), F32),
                    out_specs=pl.BlockSpec((tm, LANE), lambda i, j, k: (i, 0)),
                    extra=(b_gate,), extra_specs=(pl.BlockSpec((1, LANE), lambda i, j, k: (0, 0)),),
                    name="mlstm_gate_proj")
    g = gates[:, :n_gates].reshape(t, 4, nh).transpose(2, 1, 0)
    g = jnp.pad(g, ((0, 0), (0, 4), (0, 0)))
    h_fw = _mlstm_scan(qkv, g, seqs, cfg, reverse=False)
    hact = _mlstm_scan(qkv, g, seqs, cfg, reverse=True, h_other=h_fw, out_gate=out_gate, norm_g=norm_g)
    return _mm_residual(hact, w_out.astype(BF16), xf, cfg.mm_wide, cfg, "mlstm_out_proj")


def _rope_tables(max_len, cfg, scale):
    half = cfg.rope // 2
    inv_freq = jnp.power(cfg.rope_base, -jnp.arange(0, cfg.rope, 2, dtype=F32) / cfg.rope)
    ang = jnp.arange(max_len, dtype=F32)[:, None] * inv_freq[None, :]
    cos, sin = jnp.cos(ang), jnp.sin(ang)
    z = jnp.zeros((max_len, LANE - 2 * half), F32)
    zh = jnp.zeros((max_len, half), F32)
    cos_t = jnp.concatenate([cos, cos, z], axis=1) * scale
    sin_a = jnp.concatenate([-sin, zh, z], axis=1) * scale
    sin_b = jnp.concatenate([zh, sin, z], axis=1) * scale
    return cos_t, sin_a, sin_b


def _rotate(x, cos_t, sin_a, sin_b, half):
    return x * cos_t + pltpu.roll(x, LANE - half, 1) * sin_a + pltpu.roll(x, half, 1) * sin_b


def _ep_mla_in(acc, extra, outs, ij, *, q_rank, kv_rank, eps):
    gq_ref, gkv_ref = extra
    cq_ref, ckv_ref, kr_ref = outs

    def rms(x, g):
        return x * lax.rsqrt(jnp.mean(x * x, axis=-1, keepdims=True) + eps) * g

    cq_ref[...] = rms(acc[:, :q_rank], gq_ref[...]).astype(cq_ref.dtype)
    ckv_ref[...] = rms(acc[:, q_rank:q_rank + kv_rank], gkv_ref[...]).astype(ckv_ref.dtype)
    kr_ref[...] = acc[:, q_rank + kv_rank:]


def _ep_q_up(acc, extra, outs, ij, *, heads, scale, half):
    cos_ref, sa_ref, sb_ref = extra
    out = outs[0]
    for hh in range(heads):
        lo = hh * 2 * LANE
        out[:, lo:lo + LANE] = (acc[:, lo:lo + LANE] * scale).astype(out.dtype)
        r = acc[:, lo + LANE:lo + 2 * LANE]
        out[:, lo + LANE:lo + 2 * LANE] = _rotate(r, cos_ref[...], sa_ref[...], sb_ref[...], half).astype(out.dtype)


def _kv_up_body(x_ref, wk_ref, wv_ref, kr_ref, cos_ref, sa_ref, sb_ref, k_out, v_out, *, heads, half):
    x = x_ref[...]
    kk = jnp.dot(x, wk_ref[...], preferred_element_type=F32)
    vv = jnp.dot(x, wv_ref[...], preferred_element_type=F32)
    kr = _rotate(kr_ref[...], cos_ref[...], sa_ref[...], sb_ref[...], half).astype(k_out.dtype)
    ones = jnp.ones((x.shape[0], LANE), v_out.dtype)
    for hh in range(heads):
        lo = hh * 2 * LANE
        k_out[:, lo:lo + LANE] = kk[:, hh * LANE:(hh + 1) * LANE].astype(k_out.dtype)
        k_out[:, lo + LANE:lo + 2 * LANE] = kr
        v_out[:, lo:lo + LANE] = vv[:, hh * LANE:(hh + 1) * LANE].astype(v_out.dtype)
        v_out[:, lo + LANE:lo + 2 * LANE] = ones


def _pos_block(seqs, tm):
    def f(i):
        out = i
        for s, _ in seqs:
            out = jnp.where(i >= s // tm, i - s // tm, out)
        return out
    return f


def _attn_body(q_ref, k_ref, v_ref, o_ref, m_scr, acc_scr, *, nkv, heads, qw, vd):
    ki = pl.program_id(3)

    @pl.when(ki == 0)
    def _():
        m_scr[...] = jnp.full_like(m_scr, -jnp.inf)
        acc_scr[...] = jnp.zeros_like(acc_scr)

    tkv = k_ref.shape[0]
    for hh in range(heads):
        q = q_ref[:, hh * qw:(hh + 1) * qw]
        k = k_ref[:, hh * qw:(hh + 1) * qw]
        v = v_ref[:, hh * 2 * vd:(hh + 1) * 2 * vd]
        s = lax.dot_general(q, k, (((1,), (1,)), ((), ())), preferred_element_type=F32)
        m_prev = m_scr[hh]
        m_new = jnp.maximum(m_prev, jnp.broadcast_to(jnp.max(s, axis=-1, keepdims=True), m_prev.shape))
        alpha = jnp.exp2(m_prev - m_new)
        p = jnp.exp2(s - jnp.concatenate([m_new] * (tkv // LANE), axis=1))
        pv = jnp.dot(p.astype(v.dtype), v, preferred_element_type=F32)
        acc_scr[hh] = jnp.concatenate([alpha] * (2 * vd // LANE), axis=1) * acc_scr[hh] + pv
        m_scr[hh] = m_new

    @pl.when(ki == nkv - 1)
    def _():
        for hh in range(heads):
            o_ref[:, hh * vd:(hh + 1) * vd] = (acc_scr[hh, :, :vd] / acc_scr[hh, :, vd:]).astype(o_ref.dtype)


def _attention(q, k, v, row0, nseq, slen, cfg):
    nh, vd, qw = cfg.mla_heads, cfg.vdim, cfg.qk_pad
    assert vd == LANE
    hp = min(cfg.attn_heads_per_step, nh)
    tq, tkv = min(cfg.tq, slen), min(cfg.tkv, slen)
    assert slen % tq == 0 and slen % tkv == 0 and row0 % tq == 0 and row0 % tkv == 0 and nh % hp == 0
    nq, nkv = slen // tq, slen // tkv
    q0, k0 = row0 // tq, row0 // tkv
    return pl.pallas_call(
        functools.partial(_attn_body, nkv=nkv, heads=hp, qw=qw, vd=vd),
        grid=(nseq, nh // hp, nq, nkv),
        in_specs=[pl.BlockSpec((tq, hp * qw), lambda b, h, qi, ki: (q0 + b * nq + qi, h)),
                  pl.BlockSpec((tkv, hp * qw), lambda b, h, qi, ki: (k0 + b * nkv + ki, h)),
                  pl.BlockSpec((tkv, hp * 2 * vd), lambda b, h, qi, ki: (k0 + b * nkv + ki, h))],
        out_specs=pl.BlockSpec((tq, hp * vd), lambda b, h, qi, ki: (b * nq + qi, h)),
        out_shape=jax.ShapeDtypeStruct((nseq * slen, nh * vd), BF16),
        scratch_shapes=[pltpu.VMEM((hp, tq, LANE), F32), pltpu.VMEM((hp, tq, 2 * vd), F32)],
        compiler_params=_params(("parallel", "parallel", "parallel", "arbitrary")),
        name="mla_attention",
    )(q, k, v)


def _mla_mixer(xf, xb, w_in, q_norm_g, kv_norm_g, w_uq, w_ukv, w_out, seqs, cfg):
    t, d = xf.shape
    nh, nope, rope, vd = cfg.mla_heads, cfg.nope, cfg.rope, cfg.vdim
    qr, kvr = cfg.q_rank, cfg.kv_rank
    half = rope // 2
    assert nope == LANE and vd == LANE and rope <= LANE
    qw = cfg.qk_pad
    tm = min(cfg.mm_q_up[0], t)
    tm_in = min(cfg.mm_mla_in[0], t)
    max_len = max(n for _, n in seqs)
    pos = _pos_block(seqs, tm)

    w_in_p = jnp.pad(w_in, ((0, 0), (0, LANE - rope))).astype(BF16)
    w_uq_p = jnp.pad(w_uq.reshape(qr, nh, nope + rope), ((0, 0), (0, 0), (0, qw - nope - rope)))
    w_uq_p = w_uq_p.reshape(qr, nh * qw).astype(BF16)
    w_ukv_h = w_ukv.reshape(kvr, nh, nope + vd)
    w_uk = w_ukv_h[:, :, :nope].reshape(kvr, nh * nope).astype(BF16)
    w_uv = w_ukv_h[:, :, nope:].reshape(kvr, nh * vd).astype(BF16)

    n_in = qr + kvr + LANE
    cq, ckv, kr = _matmul(
        xb, w_in_p, tm=tm_in, tn=n_in, tk=cfg.mm_mla_in[2],
        epilogue=functools.partial(_ep_mla_in, q_rank=qr, kv_rank=kvr, eps=cfg.rms_eps),
        out_shape=[jax.ShapeDtypeStruct((t, qr), BF16), jax.ShapeDtypeStruct((t, kvr), BF16),
                   jax.ShapeDtypeStruct((t, LANE), F32)],
        out_specs=[pl.BlockSpec((tm_in, qr), lambda i, j, k: (i, 0)),
                   pl.BlockSpec((tm_in, kvr), lambda i, j, k: (i, 0)),
                   pl.BlockSpec((tm_in, LANE), lambda i, j, k: (i, 0))],
        extra=(q_norm_g.reshape(1, qr), kv_norm_g.reshape(1, kvr)),
        extra_specs=(pl.BlockSpec((1, qr), lambda i, j, k: (0, 0)),
                     pl.BlockSpec((1, kvr), lambda i, j, k: (0, 0))),
        name="mla_in_proj")

    scale = (nope + rope) ** -0.5 * LOG2_E
    q_tabs = _rope_tables(max_len, cfg, scale)
    k_tabs = _rope_tables(max_len, cfg, 1.0)
    tn = min(cfg.mm_q_up[1], nh * qw)
    tab3 = pl.BlockSpec((tm, LANE), lambda i, j, k: (pos(i), 0))
    q = _matmul(cq, w_uq_p, tm=tm, tn=tn, tk=cfg.mm_q_up[2],
                epilogue=functools.partial(_ep_q_up, heads=tn // qw, scale=scale, half=half),
                out_shape=jax.ShapeDtypeStruct((t, nh * qw), BF16),
                out_specs=pl.BlockSpec((tm, tn), lambda i, j, k: (i, j)),
                extra=q_tabs, extra_specs=(tab3, tab3, tab3), name="mla_q_up")

    hb = min(cfg.kv_heads_per_step, nh)
    tab2 = pl.BlockSpec((tm, LANE), lambda i, j: (pos(i), 0))
    k, v = pl.pallas_call(
        functools.partial(_kv_up_body, heads=hb, half=half),
        grid=(t // tm, nh // hb),
        in_specs=[pl.BlockSpec((tm, kvr), lambda i, j: (i, 0)),
                  pl.BlockSpec((kvr, hb * nope), lambda i, j: (0, j)),
                  pl.BlockSpec((kvr, hb * vd), lambda i, j: (0, j)),
                  pl.BlockSpec((tm, LANE), lambda i, j: (i, 0)),
                  tab2, tab2, tab2],
        out_specs=[pl.BlockSpec((tm, hb * qw), lambda i, j: (i, j)),
                   pl.BlockSpec((tm, hb * 2 * vd), lambda i, j: (i, j))],
        out_shape=[jax.ShapeDtypeStruct((t, nh * qw), BF16), jax.ShapeDtypeStruct((t, nh * 2 * vd), BF16)],
        compiler_params=_params(("parallel", "parallel")),
        name="mla_kv_up",
    )(ckv, w_uk, w_uv, kr, *k_tabs)

    outs = []
    idx = 0
    while idx < len(seqs):
        row0, slen = seqs[idx]
        nseq = 1
        while idx + nseq < len(seqs) and seqs[idx + nseq] == (row0 + nseq * slen, slen):
            nseq += 1
        outs.append(_attention(q, k, v, row0, nseq, slen, cfg))
        idx += nseq
    o = outs[0] if len(outs) == 1 else jnp.concatenate(outs, axis=0)
    return _mm_residual(o, w_out.astype(BF16), xf, cfg.mm_wide, cfg, "mla_out_proj")


def _ffn_up_body(x_ref, wg_ref, wu_ref, hg_ref, hu_ref, cwg_ref, cwu_ref, cbg_ref, cbu_ref, out_ref, *,
                 tm, seq_starts, seq_ends):
    r0 = pl.program_id(0) * tm
    at_start = functools.reduce(jnp.logical_or, [r0 == s for s in seq_starts])
    at_end = functools.reduce(jnp.logical_or, [r0 + tm == e for e in seq_ends])
    x = x_ref[...]

    def conv(w_ref, halo_ref, cw_ref, cb_ref):
        h = jnp.dot(x, w_ref[...], preferred_element_type=F32)
        prev_row = jnp.where(at_start, 0.0, halo_ref[0:1, :])
        next_row = jnp.where(at_end, 0.0, halo_ref[1:2, :])
        h_prev = jnp.concatenate([prev_row, h[:tm - 1, :]], axis=0)
        h_next = jnp.concatenate([h[1:, :], next_row], axis=0)
        cw = cw_ref[...]
        return h_prev * cw[0:1, :] + h * cw[1:2, :] + h_next * cw[2:3, :] + cb_ref[...]

    g = conv(wg_ref, hg_ref, cwg_ref, cbg_ref)
    u = conv(wu_ref, hu_ref, cwu_ref, cbu_ref)
    out_ref[...] = (g * jax.nn.sigmoid(g) * u).astype(out_ref.dtype)


def _conv_ffn(xf, xb, w_up, conv_w, conv_b, w_down, seqs, cfg):
    t, d = xb.shape
    f, fp, cbw = cfg.d_ff, cfg.d_ff_pad, cfg.ff_block
    pad = fp - f
    nb = fp // cbw
    assert fp % cbw == 0

    def pad_halves(a):
        lead = [(0, 0)] * (a.ndim - 1)
        return jnp.concatenate([jnp.pad(a[..., :f], lead + [(0, pad)]), jnp.pad(a[..., f:], lead + [(0, pad)])], -1)

    w_up_p = pad_halves(w_up.astype(BF16))
    w_down_p = jnp.pad(w_down.astype(BF16), ((0, pad), (0, 0)))
    cw, cb = pad_halves(conv_w), pad_halves(conv_b.reshape(1, -1))

    tm, _, tk = cfg.mm_wide
    tm = min(tm, t)
    nt = t // tm
    assert tk == d
    halo_idx = np.array([[max(i * tm - 1, 0), min((i + 1) * tm, t - 1)] * 4 for i in range(nt)], np.int32).reshape(-1)
    h_halo = _mm_plain(xb[halo_idx], w_up_p, (nt * 8, cfg.mm_halo_tn, tk), F32, "ffn_up_halo")

    def gate_cols(rows):
        return pl.BlockSpec((rows, cbw), lambda i, j: (0, j))

    def up_cols(rows):
        return pl.BlockSpec((rows, cbw), lambda i, j: (0, nb + j))

    act = pl.pallas_call(
        functools.partial(_ffn_up_body, tm=tm,
                          seq_starts=tuple(s for s, _ in seqs), seq_ends=tuple(s + n for s, n in seqs)),
        grid=(nt, nb),
        in_specs=[pl.BlockSpec((tm, d), lambda i, j: (i, 0)),
                  gate_cols(d), up_cols(d),
                  pl.BlockSpec((8, cbw), lambda i, j: (i, j)), pl.BlockSpec((8, cbw), lambda i, j: (i, nb + j)),
                  gate_cols(3), up_cols(3), gate_cols(1), up_cols(1)],
        out_specs=pl.BlockSpec((tm, cbw), lambda i, j: (i, j)),
        out_shape=jax.ShapeDtypeStruct((t, fp), BF16),
        compiler_params=_params(("parallel", "parallel")),
        name="ffn_up_conv_gate",
    )(xb, w_up_p, w_up_p, h_halo, h_halo, cw, cw, cb, cb)
    return _mm_residual(act, w_down_p, xf, cfg.mm_down, cfg, "ffn_down")


def _trunk(xf, xb, n_first, seqs, cfg, ml_w_in, ml_b_gates, ml_norm_g, ml_w_out, mla_w_in, mla_q_norm_g,
           mla_kv_norm_g, mla_w_uq, mla_w_ukv, mla_w_out, ffn_w_up, ffn_conv_w, ffn_conv_b, ffn_w_down,
           ln1_g, ln1_b, ln2_g, ln2_b):
    for i in range(cfg.depth):
        j = i // 2
        if i % 2 == 0:
            y = _mlstm_mixer(xf, xb, ml_w_in[j], ml_b_gates[j], ml_norm_g[j], ml_w_out[j], seqs, cfg)
        else:
            y = _mla_mixer(xf, xb, mla_w_in[j], mla_q_norm_g[j], mla_kv_norm_g[j],
                           mla_w_uq[j], mla_w_ukv[j], mla_w_out[j], seqs, cfg)
        xf, xb = _layer_norm(y, ln1_g[i], ln1_b[i], cfg)
        y = _conv_ffn(xf, xb, ffn_w_up[i], ffn_conv_w[i], ffn_conv_b[i], ffn_w_down[i], seqs, cfg)
        if i == cfg.depth - 1:
            return _layer_norm_split(y, ln2_g[i], ln2_b[i], n_first, cfg)
        xf, xb = _layer_norm(y, ln2_g[i], ln2_b[i], cfg)


def _run(cfg, x_prompt, x_sample, *weights):
    d = x_prompt.shape[-1]
    shapes = [x_prompt.shape, x_sample.shape]
    seqs, row = [], 0
    for b, s, _ in shapes:
        for _ in range(b):
            seqs.append((row, s))
            row += s
    xf, xb = _stream_in(x_prompt.reshape(-1, d), x_sample.reshape(-1, d), cfg)
    n_prompt = shapes[0][0] * shapes[0][1]
    y_prompt, y_sample = _trunk(xf, xb, n_prompt, tuple(seqs), cfg, *weights)
    return y_prompt.reshape(shapes[0]), y_sample.reshape(shapes[1])


def kernel(x_prompt, x_sample, ml_w_in, ml_b_gates, ml_norm_g, ml_w_out, mla_w_in, mla_q_norm_g, mla_kv_norm_g, mla_w_uq, mla_w_ukv, mla_w_out, ffn_w_up, ffn_conv_w, ffn_conv_b, ffn_w_down, ln1_g, ln1_b, ln2_g, ln2_b):
    return _run(Cfg(), x_prompt, x_sample, ml_w_in, ml_b_gates, ml_norm_g, ml_w_out, mla_w_in, mla_q_norm_g,
                mla_kv_norm_g, mla_w_uq, mla_w_ukv, mla_w_out, ffn_w_up, ffn_conv_w, ffn_conv_b, ffn_w_down,
                ln1_g, ln1_b, ln2_g, ln2_b)
```

```python
import functools
from typing import NamedTuple

import jax
import jax.numpy as jnp
import numpy as np
from jax import lax
from jax.experimental import pallas as pl
from jax.experimental.pallas import tpu as pltpu

F32 = jnp.float32
BF16 = jnp.bfloat16

LANE = 128
LOG2_E = 1.4426950408889634
VMEM_LIMIT = 56 * 1024 * 1024


class Cfg(NamedTuple):
    d_model: int = 4096
    depth: int = 4
    ml_heads: int = 8
    ml_dqk: int = 256
    ml_dv: int = 512
    ml_chunk: int = 512
    ml_m_init: float = -1e30
    mla_heads: int = 32
    nope: int = 128
    rope: int = 64
    vdim: int = 128
    q_rank: int = 1024
    kv_rank: int = 512
    rope_base: float = 10000.0
    d_ff: int = 11008
    d_ff_pad: int = 11264
    ln_eps: float = 1e-5
    rms_eps: float = 1e-6
    mm_wide: tuple = (1024, 1024, 4096)
    mm_down: tuple = (1024, 1024, 2816)
    mm_mla_in: tuple = (512, 0, 4096)
    mm_q_up: tuple = (1024, 1024, 1024)
    mm_halo_tn: int = 1024
    ff_block: int = 512
    t_row: int = 256
    t_wprep: int = 128
    tq: int = 1024
    tkv: int = 1024
    attn_heads_per_step: int = 8
    kv_heads_per_step: int = 4

    @property
    def dn_alpha(self):
        return (2 * self.depth) ** 0.25

    @property
    def qk_pad(self):
        return 2 * LANE


def _params(sem, vmem=VMEM_LIMIT):
    return pltpu.CompilerParams(dimension_semantics=sem, vmem_limit_bytes=vmem)


def _mm_body(*refs, nk, n_extra, epilogue):
    x_ref, w_ref = refs[0], refs[1]
    extra = refs[2:2 + n_extra]
    ij, k = (pl.program_id(0), pl.program_id(1)), pl.program_id(2)
    part = jnp.dot(x_ref[...], w_ref[...], preferred_element_type=F32)
    if nk == 1:
        epilogue(part, extra, refs[2 + n_extra:], ij)
        return
    outs, acc_ref = refs[2 + n_extra:-1], refs[-1]

    @pl.when(k == 0)
    def _():
        acc_ref[...] = part

    @pl.when(jnp.logical_and(k > 0, k < nk - 1))
    def _():
        acc_ref[...] += part

    @pl.when(k == nk - 1)
    def _():
        epilogue(acc_ref[...] + part, extra, outs, ij)


def _matmul(x, w, *, tm, tn, tk, epilogue, out_shape, out_specs, extra=(), extra_specs=(), name):
    m, kdim = x.shape
    n = w.shape[1]
    tm, tn, tk = min(tm, m), min(tn, n), min(tk, kdim)
    assert m % tm == 0 and n % tn == 0 and kdim % tk == 0, (x.shape, w.shape, tm, tn, tk)
    nk = kdim // tk
    return pl.pallas_call(
        functools.partial(_mm_body, nk=nk, n_extra=len(extra), epilogue=epilogue),
        grid=(m // tm, n // tn, nk),
        in_specs=[pl.BlockSpec((tm, tk), lambda i, j, k: (i, k)),
                  pl.BlockSpec((tk, tn), lambda i, j, k: (k, j)),
                  *extra_specs],
        out_specs=out_specs,
        out_shape=out_shape,
        scratch_shapes=[] if nk == 1 else [pltpu.VMEM((tm, tn), F32)],
        compiler_params=_params(("parallel", "parallel", "arbitrary")),
        name=name,
    )(x, w, *extra)


def _ep_store(acc, extra, outs, ij):
    outs[0][...] = acc.astype(outs[0].dtype)


def _ep_bias(acc, extra, outs, ij):
    outs[0][...] = acc + extra[0][...]


def _ep_sigmoid(acc, extra, outs, ij):
    outs[0][...] = jax.nn.sigmoid(acc).astype(outs[0].dtype)


def _mm_plain(x, w, tiles, out_dtype, name):
    m, n = x.shape[0], w.shape[1]
    tm, tn, tk = tiles
    tm, tn = min(tm, m), min(tn, n)
    return _matmul(x, w, tm=tm, tn=tn, tk=tk, epilogue=_ep_store,
                   out_shape=jax.ShapeDtypeStruct((m, n), out_dtype),
                   out_specs=pl.BlockSpec((tm, tn), lambda i, j, k: (i, j)), name=name)


def _mm_resid_body(x_ref, w_ref, r_ref, o_ref, *, alpha):
    @pl.when(pl.program_id(2) == 0)
    def _():
        o_ref[...] = alpha * r_ref[...]

    o_ref[...] += jnp.dot(x_ref[...], w_ref[...], preferred_element_type=F32)


def _mm_residual(x, w, resid, tiles, cfg, name):
    m, kdim = x.shape
    n = w.shape[1]
    tm, tn, tk = tiles
    tm, tn, tk = min(tm, m), min(tn, n), min(tk, kdim)
    assert m % tm == 0 and n % tn == 0 and kdim % tk == 0
    spec = pl.BlockSpec((tm, tn), lambda i, j, k: (i, j))
    return pl.pallas_call(
        functools.partial(_mm_resid_body, alpha=cfg.dn_alpha),
        grid=(m // tm, n // tn, kdim // tk),
        in_specs=[pl.BlockSpec((tm, tk), lambda i, j, k: (i, k)),
                  pl.BlockSpec((tk, tn), lambda i, j, k: (k, j)), spec],
        out_specs=spec,
        out_shape=jax.ShapeDtypeStruct((m, n), F32),
        compiler_params=_params(("parallel", "parallel", "arbitrary")),
        name=name,
    )(x, w, resid)


def _ln_body(y_ref, g_ref, b_ref, xf_ref, xb_ref, *, eps):
    y = y_ref[...]
    mu = jnp.mean(y, axis=-1, keepdims=True)
    d = y - mu
    var = jnp.mean(d * d, axis=-1, keepdims=True)
    out = d * lax.rsqrt(var + eps) * g_ref[...] + b_ref[...]
    xf_ref[...] = out
    xb_ref[...] = out.astype(BF16)


def _layer_norm(y, g, b, cfg):
    t, d = y.shape
    tr = min(cfg.t_row, t)
    row = pl.BlockSpec((tr, d), lambda i: (i, 0))
    vec = pl.BlockSpec((1, d), lambda i: (0, 0))
    return pl.pallas_call(
        functools.partial(_ln_body, eps=cfg.ln_eps),
        grid=(t // tr,),
        in_specs=[row, vec, vec],
        out_specs=[row, row],
        out_shape=[jax.ShapeDtypeStruct((t, d), F32), jax.ShapeDtypeStruct((t, d), BF16)],
        compiler_params=_params(("parallel",)),
        name="layer_norm",
    )(y, g.reshape(1, d), b.reshape(1, d))


def _split_specs(tr, d, na_tiles):
    return (pl.BlockSpec((tr, d), lambda i: (jnp.minimum(i, na_tiles - 1), 0)),
            pl.BlockSpec((tr, d), lambda i: (jnp.maximum(i - na_tiles, 0), 0)))


def _ln_split_body(y_ref, g_ref, b_ref, oa_ref, ob_ref, *, eps, na_tiles):
    i = pl.program_id(0)
    y = y_ref[...]
    mu = jnp.mean(y, axis=-1, keepdims=True)
    d = y - mu
    var = jnp.mean(d * d, axis=-1, keepdims=True)
    out = d * lax.rsqrt(var + eps) * g_ref[...] + b_ref[...]

    @pl.when(i < na_tiles)
    def _():
        oa_ref[...] = out

    @pl.when(i >= na_tiles)
    def _():
        ob_ref[...] = out


def _layer_norm_split(y, g, b, n_a, cfg):
    t, d = y.shape
    tr = min(cfg.t_row, t)
    assert n_a % tr == 0 and (t - n_a) % tr == 0 and 0 < n_a < t
    vec = pl.BlockSpec((1, d), lambda i: (0, 0))
    return pl.pallas_call(
        functools.partial(_ln_split_body, eps=cfg.ln_eps, na_tiles=n_a // tr),
        grid=(t // tr,),
        in_specs=[pl.BlockSpec((tr, d), lambda i: (i, 0)), vec, vec],
        out_specs=list(_split_specs(tr, d, n_a // tr)),
        out_shape=[jax.ShapeDtypeStruct((n_a, d), F32), jax.ShapeDtypeStruct((t - n_a, d), F32)],
        compiler_params=_params(("arbitrary",)),
        name="layer_norm_out",
    )(y, g.reshape(1, d), b.reshape(1, d))


def _stream_in_body(xa_ref, xb_ref, of_ref, ob_ref, *, na_tiles):
    i = pl.program_id(0)

    @pl.when(i < na_tiles)
    def _():
        of_ref[...] = xa_ref[...]
        ob_ref[...] = xa_ref[...].astype(BF16)

    @pl.when(i >= na_tiles)
    def _():
        of_ref[...] = xb_ref[...]
        ob_ref[...] = xb_ref[...].astype(BF16)


def _stream_in(xa, xb, cfg):
    (n_a, d), n_b = xa.shape, xb.shape[0]
    t = n_a + n_b
    tr = min(cfg.t_row, n_a, n_b)
    assert n_a % tr == 0 and n_b % tr == 0
    row = pl.BlockSpec((tr, d), lambda i: (i, 0))
    return pl.pallas_call(
        functools.partial(_stream_in_body, na_tiles=n_a // tr),
        grid=(t // tr,),
        in_specs=list(_split_specs(tr, d, n_a // tr)),
        out_specs=[row, row],
        out_shape=[jax.ShapeDtypeStruct((t, d), F32), jax.ShapeDtypeStruct((t, d), BF16)],
        compiler_params=_params(("arbitrary",)),
        name="stream_in",
    )(xa, xb)


def _mlstm_body(q_ref, k_ref, v_ref, g_ref, *rest, chunk, nc, reverse, reset_chunks, m_init, k_scale, combine_eps):
    if combine_eps is None:
        h_ref, c_scr, n_scr, m_scr = rest
    else:
        hf_ref, og_ref, ng_ref, h_ref, c_scr, n_scr, m_scr = rest
    c = pl.program_id(1)
    cc = nc - 1 - c if reverse else c

    @pl.when(functools.reduce(jnp.logical_or, [cc == r for r in reset_chunks]))
    def _():
        c_scr[...] = jnp.zeros_like(c_scr)
        n_scr[...] = jnp.zeros_like(n_scr)
        m_scr[...] = jnp.full_like(m_scr, m_init)

    gates = g_ref[0]
    ri, rf = (2, 3) if reverse else (0, 1)
    i_row = gates[ri:ri + 1, :]
    lf_row = jax.nn.log_sigmoid(gates[rf:rf + 1, :])

    row = lax.broadcasted_iota(jnp.int32, (chunk, chunk), 0)
    col = lax.broadcasted_iota(jnp.int32, (chunk, chunk), 1)
    visible = (col >= row) if reverse else (col <= row)
    visible_t = (row >= col) if reverse else (row <= col)
    b_col = jnp.sum(jnp.where(visible, lf_row, 0.0), axis=-1, keepdims=True)
    lf_col = jnp.sum(jnp.where(row == col, lf_row, 0.0), axis=-1, keepdims=True)
    b_row = jnp.sum(jnp.where(visible_t, lf_col, 0.0), axis=0, keepdims=True)

    m_prev = m_scr[0:1, 0:1]
    logd = jnp.where(visible, b_col - b_row + i_row, -jnp.inf)
    inter = b_col + m_prev
    m_row = jnp.maximum(jnp.max(logd, axis=-1, keepdims=True), inter)
    dmat = jnp.exp(logd - m_row)

    q = q_ref[...]
    k_f32 = k_ref[...].astype(F32) * k_scale
    k = k_f32.astype(BF16)
    v = v_ref[...]
    s = lax.dot_general(q, k, (((1,), (1,)), ((), ())), preferred_element_type=F32) * dmat
    w_inter = jnp.exp(inter - m_row)
    ct = c_scr[...]
    qc = jnp.dot(q, ct.astype(BF16), preferred_element_type=F32)
    num = jnp.dot(s.astype(BF16), v, preferred_element_type=F32) + w_inter * qc
    qn = jnp.dot(q, n_scr[...].astype(BF16), preferred_element_type=F32)[:, 0:1]
    den = jnp.sum(s, axis=-1, keepdims=True) + w_inter * qn
    h = num / jnp.maximum(jnp.abs(den), jnp.exp(-m_row))
    if combine_eps is None:
        h_ref[...] = h
    else:
        h = h + hf_ref[...]
        h = h * lax.rsqrt(jnp.mean(h * h, axis=-1, keepdims=True) + combine_eps)
        h_ref[...] = (h * ng_ref[...] * og_ref[...].astype(F32)).astype(h_ref.dtype)

    bl = jnp.sum(lf_row, axis=-1, keepdims=True)
    a_row = bl - b_row + i_row
    m_new = jnp.maximum(bl + m_prev, jnp.max(a_row, axis=-1, keepdims=True))
    decay = jnp.exp(bl + m_prev - m_new)
    wk_t = k_f32.T * jnp.exp(a_row - m_new)
    c_scr[...] = decay * ct + jnp.dot(wk_t.astype(BF16), v, preferred_element_type=F32)
    n_scr[...] = decay * n_scr[...] + jnp.sum(wk_t, axis=-1, keepdims=True)
    m_scr[...] = jnp.broadcast_to(m_new, m_scr.shape)


def _mlstm_scan(qkv, gates, seqs, cfg, *, reverse, h_other=None, out_gate=None, norm_g=None):
    t = qkv.shape[0]
    nh, dqk, dv, chunk = cfg.ml_heads, cfg.ml_dqk, cfg.ml_dv, cfg.ml_chunk
    chunk = min(chunk, min(n for _, n in seqs))
    assert all(s % chunk == 0 and n % chunk == 0 for s, n in seqs)
    nc = t // chunk
    if reverse:
        resets = tuple((s + n) // chunk - 1 for s, n in seqs)
    else:
        resets = tuple(s // chunk for s, _ in seqs)

    def ci(c):
        return nc - 1 - c if reverse else c

    assert (2 * nh * dqk) % dv == 0
    v_off = 2 * nh * dqk // dv
    combine = h_other is not None
    h_spec = pl.BlockSpec((chunk, dv), lambda h, c: (ci(c), h))
    in_specs = [pl.BlockSpec((chunk, dqk), lambda h, c: (ci(c), h)),
                pl.BlockSpec((chunk, dqk), lambda h, c: (ci(c), nh + h)),
                pl.BlockSpec((chunk, dv), lambda h, c: (ci(c), v_off + h)),
                pl.BlockSpec((1, 8, chunk), lambda h, c: (h, 0, ci(c)))]
    operands = [qkv, qkv, qkv, gates]
    if combine:
        in_specs += [h_spec, h_spec, pl.BlockSpec((1, dv), lambda h, c: (0, h))]
        operands += [h_other, out_gate, norm_g.reshape(1, nh * dv)]
    return pl.pallas_call(
        functools.partial(_mlstm_body, chunk=chunk, nc=nc, reverse=reverse, reset_chunks=resets,
                          m_init=cfg.ml_m_init, k_scale=dqk ** -0.5,
                          combine_eps=cfg.rms_eps if combine else None),
        grid=(nh, nc),
        in_specs=in_specs,
        out_specs=h_spec,
        out_shape=jax.ShapeDtypeStruct((t, nh * dv), BF16 if combine else F32),
        scratch_shapes=[pltpu.VMEM((dqk, dv), F32), pltpu.VMEM((dqk, LANE), F32), pltpu.VMEM((8, LANE), F32)],
        compiler_params=_params(("parallel", "arbitrary")),
        name="mlstm_scan_bw" if reverse else "mlstm_scan_fw",
    )(*operands)


def _mlstm_mixer(xf, xb, w_in, b_gates, norm_g, w_out, seqs, cfg):
    t = xf.shape[0]
    nh, dqk, dv = cfg.ml_heads, cfg.ml_dqk, cfg.ml_dv
    qk_w, v_w = nh * dqk, nh * dv
    qkv_w = 2 * qk_w + v_w
    main_w = qkv_w + v_w
    n_gates = 4 * nh
    w_qkv = w_in[:, :qkv_w].astype(BF16)
    w_o = w_in[:, qkv_w:main_w].astype(BF16)
    w_gate = jnp.pad(w_in[:, main_w:], ((0, 0), (0, LANE - n_gates))).astype(BF16)
    b_gate = jnp.pad(b_gates, (0, LANE - n_gates)).reshape(1, LANE)

    tm, tn, tk = cfg.mm_wide
    tm, tn = min(tm, t), min(tn, v_w)
    qkv = _mm_plain(xb, w_qkv, cfg.mm_wide, BF16, "mlstm_in_proj")
    out_gate = _matmul(xb, w_o, tm=tm, tn=tn, tk=tk, epilogue=_ep_sigmoid,
                       out_shape=jax.ShapeDtypeStruct((t, v_w), BF16),
                       out_specs=pl.BlockSpec((tm, tn), lambda i, j, k: (i, j)), name="mlstm_out_gate_proj")
    gates = _matmul(xb, w_gate, tm=tm, tn=LANE, tk=tk, epilogue=_ep_bias,
                    out_shape=jax.ShapeDtypeStruct((t, LANE), F32),
                    out_specs=pl.BlockSpec((tm, LANE), lambda i, j, k: (i, 0)),
                    extra=(b_gate,), extra_specs=(pl.BlockSpec((1, LANE), lambda i, j, k: (0, 0)),),
                    name="mlstm_gate_proj")
    g = gates[:, :n_gates].reshape(t, 4, nh).transpose(2, 1, 0)
    g = jnp.pad(g, ((0, 0), (0, 4), (0, 0)))
    h_fw = _mlstm_scan(qkv, g, seqs, cfg, reverse=False)
    hact = _mlstm_scan(qkv, g, seqs, cfg, reverse=True, h_other=h_fw, out_gate=out_gate, norm_g=norm_g)
    return _mm_residual(hact, w_out.astype(BF16), xf, cfg.mm_wide, cfg, "mlstm_out_proj")


def _rope_tables(max_len, cfg, scale):
    half = cfg.rope // 2
    inv_freq = jnp.power(cfg.rope_base, -jnp.arange(0, cfg.rope, 2, dtype=F32) / cfg.rope)
    ang = jnp.arange(max_len, dtype=F32)[:, None] * inv_freq[None, :]
    cos, sin = jnp.cos(ang), jnp.sin(ang)
    z = jnp.zeros((max_len, LANE - 2 * half), F32)
    zh = jnp.zeros((max_len, half), F32)
    cos_t = jnp.concatenate([cos, cos, z], axis=1) * scale
    sin_a = jnp.concatenate([-sin, zh, z], axis=1) * scale
    sin_b = jnp.concatenate([zh, sin, z], axis=1) * scale
    return cos_t, sin_a, sin_b


def _rotate(x, cos_t, sin_a, sin_b, half):
    return x * cos_t + pltpu.roll(x, LANE - half, 1) * sin_a + pltpu.roll(x, half, 1) * sin_b


def _ep_mla_in(acc, extra, outs, ij, *, q_rank, kv_rank, eps):
    gq_ref, gkv_ref = extra
    cq_ref, ckv_ref, kr_ref = outs

    def rms(x, g):
        return x * lax.rsqrt(jnp.mean(x * x, axis=-1, keepdims=True) + eps) * g

    cq_ref[...] = rms(acc[:, :q_rank], gq_ref[...]).astype(cq_ref.dtype)
    ckv_ref[...] = rms(acc[:, q_rank:q_rank + kv_rank], gkv_ref[...]).astype(ckv_ref.dtype)
    kr_ref[...] = acc[:, q_rank + kv_rank:]


def _ep_q_up(acc, extra, outs, ij, *, heads, scale, half):
    cos_ref, sa_ref, sb_ref = extra
    out = outs[0]
    for hh in range(heads):
        lo = hh * 2 * LANE
        out[:, lo:lo + LANE] = (acc[:, lo:lo + LANE] * scale).astype(out.dtype)
        r = acc[:, lo + LANE:lo + 2 * LANE]
        out[:, lo + LANE:lo + 2 * LANE] = _rotate(r, cos_ref[...], sa_ref[...], sb_ref[...], half).astype(out.dtype)


def _kv_up_body(x_ref, wk_ref, wv_ref, kr_ref, cos_ref, sa_ref, sb_ref, k_out, v_out, *, heads, half):
    x = x_ref[...]
    kk = jnp.dot(x, wk_ref[...], preferred_element_type=F32)
    vv = jnp.dot(x, wv_ref[...], preferred_element_type=F32)
    kr = _rotate(kr_ref[...], cos_ref[...], sa_ref[...], sb_ref[...], half).astype(k_out.dtype)
    ones = jnp.ones((x.shape[0], LANE), v_out.dtype)
    for hh in range(heads):
        lo = hh * 2 * LANE
        k_out[:, lo:lo + LANE] = kk[:, hh * LANE:(hh + 1) * LANE].astype(k_out.dtype)
        k_out[:, lo + LANE:lo + 2 * LANE] = kr
        v_out[:, lo:lo + LANE] = vv[:, hh * LANE:(hh + 1) * LANE].astype(v_out.dtype)
        v_out[:, lo + LANE:lo + 2 * LANE] = ones


def _pos_block(seqs, tm):
    def f(i):
        out = i
        for s, _ in seqs:
            out = jnp.where(i >= s // tm, i - s // tm, out)
        return out
    return f


def _attn_body(q_ref, k_ref, v_ref, o_ref, m_scr, acc_scr, *, nkv, heads, qw, vd):
    ki = pl.program_id(3)

    @pl.when(ki == 0)
    def _():
        m_scr[...] = jnp.full_like(m_scr, -jnp.inf)
        acc_scr[...] = jnp.zeros_like(acc_scr)

    tkv = k_ref.shape[0]
    for hh in range(heads):
        q = q_ref[:, hh * qw:(hh + 1) * qw]
        k = k_ref[:, hh * qw:(hh + 1) * qw]
        v = v_ref[:, hh * 2 * vd:(hh + 1) * 2 * vd]
        s = lax.dot_general(q, k, (((1,), (1,)), ((), ())), preferred_element_type=F32)
        m_prev = m_scr[hh]
        m_new = jnp.maximum(m_prev, jnp.broadcast_to(jnp.max(s, axis=-1, keepdims=True), m_prev.shape))
        alpha = jnp.exp2(m_prev - m_new)
        p = jnp.exp2(s - jnp.concatenate([m_new] * (tkv // LANE), axis=1))
        pv = jnp.dot(p.astype(v.dtype), v, preferred_element_type=F32)
        acc_scr[hh] = jnp.concatenate([alpha] * (2 * vd // LANE), axis=1) * acc_scr[hh] + pv
        m_scr[hh] = m_new

    @pl.when(ki == nkv - 1)
    def _():
        for hh in range(heads):
            o_ref[:, hh * vd:(hh + 1) * vd] = (acc_scr[hh, :, :vd] / acc_scr[hh, :, vd:]).astype(o_ref.dtype)


def _attention(q, k, v, row0, nseq, slen, cfg):
    nh, vd, qw = cfg.mla_heads, cfg.vdim, cfg.qk_pad
    assert vd == LANE
    hp = min(cfg.attn_heads_per_step, nh)
    tq, tkv = min(cfg.tq, slen), min(cfg.tkv, slen)
    assert slen % tq == 0 and slen % tkv == 0 and row0 % tq == 0 and row0 % tkv == 0 and nh % hp == 0
    nq, nkv = slen // tq, slen // tkv
    q0, k0 = row0 // tq, row0 // tkv
    return pl.pallas_call(
        functools.partial(_attn_body, nkv=nkv, heads=hp, qw=qw, vd=vd),
        grid=(nseq, nh // hp, nq, nkv),
        in_specs=[pl.BlockSpec((tq, hp * qw), lambda b, h, qi, ki: (q0 + b * nq + qi, h)),
                  pl.BlockSpec((tkv, hp * qw), lambda b, h, qi, ki: (k0 + b * nkv + ki, h)),
                  pl.BlockSpec((tkv, hp * 2 * vd), lambda b, h, qi, ki: (k0 + b * nkv + ki, h))],
        out_specs=pl.BlockSpec((tq, hp * vd), lambda b, h, qi, ki: (b * nq + qi, h)),
        out_shape=jax.ShapeDtypeStruct((nseq * slen, nh * vd), BF16),
        scratch_shapes=[pltpu.VMEM((hp, tq, LANE), F32), pltpu.VMEM((hp, tq, 2 * vd), F32)],
        compiler_params=_params(("parallel", "parallel", "parallel", "arbitrary")),
        name="mla_attention",
    )(q, k, v)


def _mla_mixer(xf, xb, w_in, q_norm_g, kv_norm_g, w_uq, w_ukv, w_out, seqs, cfg):
    t, d = xf.shape
    nh, nope, rope, vd = cfg.mla_heads, cfg.nope, cfg.rope, cfg.vdim
    qr, kvr = cfg.q_rank, cfg.kv_rank
    half = rope // 2
    assert nope == LANE and vd == LANE and rope <= LANE
    qw = cfg.qk_pad
    tm = min(cfg.mm_q_up[0], t)
    tm_in = min(cfg.mm_mla_in[0], t)
    max_len = max(n for _, n in seqs)
    pos = _pos_block(seqs, tm)

    w_in_p = jnp.pad(w_in, ((0, 0), (0, LANE - rope))).astype(BF16)
    w_uq_p = jnp.pad(w_uq.reshape(qr, nh, nope + rope), ((0, 0), (0, 0), (0, qw - nope - rope)))
    w_uq_p = w_uq_p.reshape(qr, nh * qw).astype(BF16)
    w_ukv_h = w_ukv.reshape(kvr, nh, nope + vd)
    w_uk = w_ukv_h[:, :, :nope].reshape(kvr, nh * nope).astype(BF16)
    w_uv = w_ukv_h[:, :, nope:].reshape(kvr, nh * vd).astype(BF16)

    n_in = qr + kvr + LANE
    cq, ckv, kr = _matmul(
        xb, w_in_p, tm=tm_in, tn=n_in, tk=cfg.mm_mla_in[2],
        epilogue=functools.partial(_ep_mla_in, q_rank=qr, kv_rank=kvr, eps=cfg.rms_eps),
        out_shape=[jax.ShapeDtypeStruct((t, qr), BF16), jax.ShapeDtypeStruct((t, kvr), BF16),
                   jax.ShapeDtypeStruct((t, LANE), F32)],
        out_specs=[pl.BlockSpec((tm_in, qr), lambda i, j, k: (i, 0)),
                   pl.BlockSpec((tm_in, kvr), lambda i, j, k: (i, 0)),
                   pl.BlockSpec((tm_in, LANE), lambda i, j, k: (i, 0))],
        extra=(q_norm_g.reshape(1, qr), kv_norm_g.reshape(1, kvr)),
        extra_specs=(pl.BlockSpec((1, qr), lambda i, j, k: (0, 0)),
                     pl.BlockSpec((1, kvr), lambda i, j, k: (0, 0))),
        name="mla_in_proj")

    scale = (nope + rope) ** -0.5 * LOG2_E
    q_tabs = _rope_tables(max_len, cfg, scale)
    k_tabs = _rope_tables(max_len, cfg, 1.0)
    tn = min(cfg.mm_q_up[1], nh * qw)
    tab3 = pl.BlockSpec((tm, LANE), lambda i, j, k: (pos(i), 0))
    q = _matmul(cq, w_uq_p, tm=tm, tn=tn, tk=cfg.mm_q_up[2],
                epilogue=functools.partial(_ep_q_up, heads=tn // qw, scale=scale, half=half),
                out_shape=jax.ShapeDtypeStruct((t, nh * qw), BF16),
                out_specs=pl.BlockSpec((tm, tn), lambda i, j, k: (i, j)),
                extra=q_tabs, extra_specs=(tab3, tab3, tab3), name="mla_q_up")

    hb = min(cfg.kv_heads_per_step, nh)
    tab2 = pl.BlockSpec((tm, LANE), lambda i, j: (pos(i), 0))
    k, v = pl.pallas_call(
        functools.partial(_kv_up_body, heads=hb, half=half),
        grid=(t // tm, nh // hb),
        in_specs=[pl.BlockSpec((tm, kvr), lambda i, j: (i, 0)),
                  pl.BlockSpec((kvr, hb * nope), lambda i, j: (0, j)),
                  pl.BlockSpec((kvr, hb * vd), lambda i, j: (0, j)),
                  pl.BlockSpec((tm, LANE), lambda i, j: (i, 0)),
                  tab2, tab2, tab2],
        out_specs=[pl.BlockSpec((tm, hb * qw), lambda i, j: (i, j)),
                   pl.BlockSpec((tm, hb * 2 * vd), lambda i, j: (i, j))],
        out_shape=[jax.ShapeDtypeStruct((t, nh * qw), BF16), jax.ShapeDtypeStruct((t, nh * 2 * vd), BF16)],
        compiler_params=_params(("parallel", "parallel")),
        name="mla_kv_up",
    )(ckv, w_uk, w_uv, kr, *k_tabs)

    outs = []
    idx = 0
    while idx < len(seqs):
        row0, slen = seqs[idx]
        nseq = 1
        while idx + nseq < len(seqs) and seqs[idx + nseq] == (row0 + nseq * slen, slen):
            nseq += 1
        outs.append(_attention(q, k, v, row0, nseq, slen, cfg))
        idx += nseq
    o = outs[0] if len(outs) == 1 else jnp.concatenate(outs, axis=0)
    return _mm_residual(o, w_out.astype(BF16), xf, cfg.mm_wide, cfg, "mla_out_proj")


def _ffn_up_body(x_ref, wg_ref, wu_ref, hg_ref, hu_ref, cwg_ref, cwu_ref, cbg_ref, cbu_ref, out_ref, *,
                 tm, seq_starts, seq_ends):
    r0 = pl.program_id(0) * tm
    at_start = functools.reduce(jnp.logical_or, [r0 == s for s in seq_starts])
    at_end = functools.reduce(jnp.logical_or, [r0 + tm == e for e in seq_ends])
    x = x_ref[...]

    def conv(w_ref, halo_ref, cw_ref, cb_ref):
        h = jnp.dot(x, w_ref[...], preferred_element_type=F32)
        prev_row = jnp.where(at_start, 0.0, halo_ref[0:1, :])
        next_row = jnp.where(at_end, 0.0, halo_ref[1:2, :])
        h_prev = jnp.concatenate([prev_row, h[:tm - 1, :]], axis=0)
        h_next = jnp.concatenate([h[1:, :], next_row], axis=0)
        cw = cw_ref[...]
        return h_prev * cw[0:1, :] + h * cw[1:2, :] + h_next * cw[2:3, :] + cb_ref[...]

    g = conv(wg_ref, hg_ref, cwg_ref, cbg_ref)
    u = conv(wu_ref, hu_ref, cwu_ref, cbu_ref)
    out_ref[...] = (g * jax.nn.sigmoid(g) * u).astype(out_ref.dtype)


def _up_weight_body(w_ref, o_ref, *, f, fp):
    rows = w_ref.shape[0]
    o_ref[:, 0:f] = w_ref[:, 0:f].astype(BF16)
    o_ref[:, f:fp] = jnp.zeros((rows, fp - f), BF16)
    o_ref[:, fp:fp + f] = w_ref[:, f:2 * f].astype(BF16)
    o_ref[:, fp + f:2 * fp] = jnp.zeros((rows, fp - f), BF16)


def _down_weight_body(w_ref, o_ref, *, n_src_tiles):
    i = pl.program_id(0)

    @pl.when(i < n_src_tiles)
    def _():
        o_ref[...] = w_ref[...].astype(BF16)

    @pl.when(i >= n_src_tiles)
    def _():
        o_ref[...] = jnp.zeros_like(o_ref)


def _ffn_weights(w_up, w_down, layer, cfg):
    _, d, _ = w_up.shape
    f, fp = cfg.d_ff, cfg.d_ff_pad
    assert f % LANE == 0 and fp % LANE == 0
    tr = min(cfg.t_wprep, d)
    assert d % tr == 0
    up = pl.pallas_call(
        functools.partial(_up_weight_body, f=f, fp=fp),
        grid=(d // tr,),
        in_specs=[pl.BlockSpec((None, tr, 2 * f), lambda i: (layer, i, 0))],
        out_specs=pl.BlockSpec((tr, 2 * fp), lambda i: (i, 0)),
        out_shape=jax.ShapeDtypeStruct((d, 2 * fp), BF16),
        compiler_params=_params(("parallel",)),
        name="ffn_up_weight_prep",
    )(w_up)

    tk = fp - f
    assert f % tk == 0 and tk % 16 == 0
    n_src = f // tk
    down = pl.pallas_call(
        functools.partial(_down_weight_body, n_src_tiles=n_src),
        grid=(fp // tk,),
        in_specs=[pl.BlockSpec((None, tk, d), lambda i: (layer, jnp.minimum(i, n_src - 1), 0))],
        out_specs=pl.BlockSpec((tk, d), lambda i: (i, 0)),
        out_shape=jax.ShapeDtypeStruct((fp, d), BF16),
        compiler_params=_params(("arbitrary",)),
        name="ffn_down_weight_prep",
    )(w_down)
    return up, down


def _conv_ffn(xf, xb, w_up_p, conv_w, conv_b, w_down_p, seqs, cfg):
    t, d = xb.shape
    f, fp, cbw = cfg.d_ff, cfg.d_ff_pad, cfg.ff_block
    pad = fp - f
    nb = fp // cbw
    assert fp % cbw == 0

    def pad_halves(a):
        lead = [(0, 0)] * (a.ndim - 1)
        return jnp.concatenate([jnp.pad(a[..., :f], lead + [(0, pad)]), jnp.pad(a[..., f:], lead + [(0, pad)])], -1)

    cw, cb = pad_halves(conv_w), pad_halves(conv_b.reshape(1, -1))

    tm, _, tk = cfg.mm_wide
    tm = min(tm, t)
    nt = t // tm
    assert tk == d
    halo_idx = np.array([[max(i * tm - 1, 0), min((i + 1) * tm, t - 1)] * 4 for i in range(nt)], np.int32).reshape(-1)
    h_halo = _mm_plain(xb[halo_idx], w_up_p, (nt * 8, cfg.mm_halo_tn, tk), F32, "ffn_up_halo")

    def gate_cols(rows):
        return pl.BlockSpec((rows, cbw), lambda i, j: (0, j))

    def up_cols(rows):
        return pl.BlockSpec((rows, cbw), lambda i, j: (0, nb + j))

    act = pl.pallas_call(
        functools.partial(_ffn_up_body, tm=tm,
                          seq_starts=tuple(s for s, _ in seqs), seq_ends=tuple(s + n for s, n in seqs)),
        grid=(nt, nb),
        in_specs=[pl.BlockSpec((tm, d), lambda i, j: (i, 0)),
                  gate_cols(d), up_cols(d),
                  pl.BlockSpec((8, cbw), lambda i, j: (i, j)), pl.BlockSpec((8, cbw), lambda i, j: (i, nb + j)),
                  gate_cols(3), up_cols(3), gate_cols(1), up_cols(1)],
        out_specs=pl.BlockSpec((tm, cbw), lambda i, j: (i, j)),
        out_shape=jax.ShapeDtypeStruct((t, fp), BF16),
        compiler_params=_params(("parallel", "parallel")),
        name="ffn_up_conv_gate",
    )(xb, w_up_p, w_up_p, h_halo, h_halo, cw, cw, cb, cb)
    return _mm_residual(act, w_down_p, xf, cfg.mm_down, cfg, "ffn_down")


def _trunk(xf, xb, n_first, seqs, cfg, ml_w_in, ml_b_gates, ml_norm_g, ml_w_out, mla_w_in, mla_q_norm_g,
           mla_kv_norm_g, mla_w_uq, mla_w_ukv, mla_w_out, ffn_w_up, ffn_conv_w, ffn_conv_b, ffn_w_down,
           ln1_g, ln1_b, ln2_g, ln2_b):
    for i in range(cfg.depth):
        j = i // 2
        if i % 2 == 0:
            y = _mlstm_mixer(xf, xb, ml_w_in[j], ml_b_gates[j], ml_norm_g[j], ml_w_out[j], seqs, cfg)
        else:
            y = _mla_mixer(xf, xb, mla_w_in[j], mla_q_norm_g[j], mla_kv_norm_g[j],
                           mla_w_uq[j], mla_w_ukv[j], mla_w_out[j], seqs, cfg)
        xf, xb = _layer_norm(y, ln1_g[i], ln1_b[i], cfg)
        w_up_p, w_down_p = _ffn_weights(ffn_w_up, ffn_w_down, i, cfg)
        y = _conv_ffn(xf, xb, w_up_p, ffn_conv_w[i], ffn_conv_b[i], w_down_p, seqs, cfg)
        if i == cfg.depth - 1:
            return _layer_norm_split(y, ln2_g[i], ln2_b[i], n_first, cfg)
        xf, xb = _layer_norm(y, ln2_g[i], ln2_b[i], cfg)


def _run(cfg, x_prompt, x_sample, *weights):
    d = x_prompt.shape[-1]
    shapes = [x_prompt.shape, x_sample.shape]
    seqs, row = [], 0
    for b, s, _ in shapes:
        for _ in range(b):
            seqs.append((row, s))
            row += s
    xf, xb = _stream_in(x_prompt.reshape(-1, d), x_sample.reshape(-1, d), cfg)
    n_prompt = shapes[0][0] * shapes[0][1]
    y_prompt, y_sample = _trunk(xf, xb, n_prompt, tuple(seqs), cfg, *weights)
    return y_prompt.reshape(shapes[0]), y_sample.reshape(shapes[1])


def kernel(x_prompt, x_sample, ml_w_in, ml_b_gates, ml_norm_g, ml_w_out, mla_w_in, mla_q_norm_g, mla_kv_norm_g, mla_w_uq, mla_w_ukv, mla_w_out, ffn_w_up, ffn_conv_w, ffn_conv_b, ffn_w_down, ln1_g, ln1_b, ln2_g, ln2_b):
    return _run(Cfg(), x_prompt, x_sample, ml_w_in, ml_b_gates, ml_norm_g, ml_w_out, mla_w_in, mla_q_norm_g,
                mla_kv_norm_g, mla_w_uq, mla_w_ukv, mla_w_out, ffn_w_up, ffn_conv_w, ffn_conv_b, ffn_w_down,
                ln1_g, ln1_b, ln2_g, ln2_b)
```

```python
import functools
from typing import NamedTuple

import jax
import jax.numpy as jnp
import numpy as np
from jax import lax
from jax.experimental import pallas as pl
from jax.experimental.pallas import tpu as pltpu

F32 = jnp.float32
BF16 = jnp.bfloat16

LANE = 128
LOG2_E = 1.4426950408889634
VMEM_LIMIT = 56 * 1024 * 1024


class Cfg(NamedTuple):
    d_model: int = 4096
    depth: int = 4
    ml_heads: int = 8
    ml_dqk: int = 256
    ml_dv: int = 512
    ml_chunk: int = 512
    ml_m_init: float = -1e30
    ml_heads_per_step: int = 1
    ml_heads_per_step_combine: int = 2
    mla_heads: int = 32
    nope: int = 128
    rope: int = 64
    vdim: int = 128
    q_rank: int = 1024
    kv_rank: int = 512
    rope_base: float = 10000.0
    d_ff: int = 11008
    d_ff_pad: int = 11264
    ln_eps: float = 1e-5
    rms_eps: float = 1e-6
    mm_wide: tuple = (1024, 1024, 4096)
    mm_down: tuple = (1024, 1024, 2816)
    mm_mla_in: tuple = (512, 0, 4096)
    mm_q_up: tuple = (1024, 1024, 1024)
    mm_halo_tn: int = 1024
    ff_block: int = 512
    t_row: int = 512
    t_wprep: int = 128
    tq: int = 1024
    tkv: int = 1024
    attn_heads_per_step: int = 8
    kv_heads_per_step: int = 4

    @property
    def dn_alpha(self):
        return (2 * self.depth) ** 0.25

    @property
    def qk_pad(self):
        return 2 * LANE


def _params(sem, vmem=VMEM_LIMIT):
    return pltpu.CompilerParams(dimension_semantics=sem, vmem_limit_bytes=vmem)


def _mm_body(*refs, nk, n_extra, epilogue):
    x_ref, w_ref = refs[0], refs[1]
    extra = refs[2:2 + n_extra]
    ij, k = (pl.program_id(0), pl.program_id(1)), pl.program_id(2)
    part = jnp.dot(x_ref[...], w_ref[...], preferred_element_type=F32)
    if nk == 1:
        epilogue(part, extra, refs[2 + n_extra:], ij)
        return
    outs, acc_ref = refs[2 + n_extra:-1], refs[-1]

    @pl.when(k == 0)
    def _():
        acc_ref[...] = part

    @pl.when(jnp.logical_and(k > 0, k < nk - 1))
    def _():
        acc_ref[...] += part

    @pl.when(k == nk - 1)
    def _():
        epilogue(acc_ref[...] + part, extra, outs, ij)


def _matmul(x, w, *, tm, tn, tk, epilogue, out_shape, out_specs, extra=(), extra_specs=(), name):
    m, kdim = x.shape
    n = w.shape[1]
    tm, tn, tk = min(tm, m), min(tn, n), min(tk, kdim)
    assert m % tm == 0 and n % tn == 0 and kdim % tk == 0, (x.shape, w.shape, tm, tn, tk)
    nk = kdim // tk
    return pl.pallas_call(
        functools.partial(_mm_body, nk=nk, n_extra=len(extra), epilogue=epilogue),
        grid=(m // tm, n // tn, nk),
        in_specs=[pl.BlockSpec((tm, tk), lambda i, j, k: (i, k)),
                  pl.BlockSpec((tk, tn), lambda i, j, k: (k, j)),
                  *extra_specs],
        out_specs=out_specs,
        out_shape=out_shape,
        scratch_shapes=[] if nk == 1 else [pltpu.VMEM((tm, tn), F32)],
        compiler_params=_params(("parallel", "parallel", "arbitrary")),
        name=name,
    )(x, w, *extra)


def _ep_store(acc, extra, outs, ij):
    outs[0][...] = acc.astype(outs[0].dtype)


def _ep_bias(acc, extra, outs, ij):
    outs[0][...] = acc + extra[0][...]


def _ep_sigmoid(acc, extra, outs, ij):
    outs[0][...] = jax.nn.sigmoid(acc).astype(outs[0].dtype)


def _mm_plain(x, w, tiles, out_dtype, name):
    m, n = x.shape[0], w.shape[1]
    tm, tn, tk = tiles
    tm, tn = min(tm, m), min(tn, n)
    return _matmul(x, w, tm=tm, tn=tn, tk=tk, epilogue=_ep_store,
                   out_shape=jax.ShapeDtypeStruct((m, n), out_dtype),
                   out_specs=pl.BlockSpec((tm, tn), lambda i, j, k: (i, j)), name=name)


def _mm_resid_body(x_ref, w_ref, r_ref, o_ref, *, alpha):
    @pl.when(pl.program_id(2) == 0)
    def _():
        o_ref[...] = alpha * r_ref[...]

    o_ref[...] += jnp.dot(x_ref[...], w_ref[...], preferred_element_type=F32)


def _mm_residual(x, w, resid, tiles, cfg, name):
    m, kdim = x.shape
    n = w.shape[1]
    tm, tn, tk = tiles
    tm, tn, tk = min(tm, m), min(tn, n), min(tk, kdim)
    assert m % tm == 0 and n % tn == 0 and kdim % tk == 0
    spec = pl.BlockSpec((tm, tn), lambda i, j, k: (i, j))
    return pl.pallas_call(
        functools.partial(_mm_resid_body, alpha=cfg.dn_alpha),
        grid=(m // tm, n // tn, kdim // tk),
        in_specs=[pl.BlockSpec((tm, tk), lambda i, j, k: (i, k)),
                  pl.BlockSpec((tk, tn), lambda i, j, k: (k, j)), spec],
        out_specs=spec,
        out_shape=jax.ShapeDtypeStruct((m, n), F32),
        compiler_params=_params(("parallel", "parallel", "arbitrary")),
        name=name,
    )(x, w, resid)


def _ln_body(y_ref, g_ref, b_ref, xf_ref, xb_ref, *, eps):
    y = y_ref[...]
    mu = jnp.mean(y, axis=-1, keepdims=True)
    d = y - mu
    var = jnp.mean(d * d, axis=-1, keepdims=True)
    out = d * lax.rsqrt(var + eps) * g_ref[...] + b_ref[...]
    xf_ref[...] = out
    xb_ref[...] = out.astype(BF16)


def _layer_norm(y, g, b, cfg):
    t, d = y.shape
    tr = min(cfg.t_row, t)
    row = pl.BlockSpec((tr, d), lambda i: (i, 0))
    vec = pl.BlockSpec((1, d), lambda i: (0, 0))
    return pl.pallas_call(
        functools.partial(_ln_body, eps=cfg.ln_eps),
        grid=(t // tr,),
        in_specs=[row, vec, vec],
        out_specs=[row, row],
        out_shape=[jax.ShapeDtypeStruct((t, d), F32), jax.ShapeDtypeStruct((t, d), BF16)],
        compiler_params=_params(("parallel",)),
        name="layer_norm",
    )(y, g.reshape(1, d), b.reshape(1, d))


def _split_specs(tr, d, na_tiles):
    return (pl.BlockSpec((tr, d), lambda i: (jnp.minimum(i, na_tiles - 1), 0)),
            pl.BlockSpec((tr, d), lambda i: (jnp.maximum(i - na_tiles, 0), 0)))


def _ln_split_body(y_ref, g_ref, b_ref, oa_ref, ob_ref, *, eps, na_tiles):
    i = pl.program_id(0)
    y = y_ref[...]
    mu = jnp.mean(y, axis=-1, keepdims=True)
    d = y - mu
    var = jnp.mean(d * d, axis=-1, keepdims=True)
    out = d * lax.rsqrt(var + eps) * g_ref[...] + b_ref[...]

    @pl.when(i < na_tiles)
    def _():
        oa_ref[...] = out

    @pl.when(i >= na_tiles)
    def _():
        ob_ref[...] = out


def _layer_norm_split(y, g, b, n_a, cfg):
    t, d = y.shape
    tr = min(cfg.t_row // 2, t)
    assert n_a % tr == 0 and (t - n_a) % tr == 0 and 0 < n_a < t
    vec = pl.BlockSpec((1, d), lambda i: (0, 0))
    return pl.pallas_call(
        functools.partial(_ln_split_body, eps=cfg.ln_eps, na_tiles=n_a // tr),
        grid=(t // tr,),
        in_specs=[pl.BlockSpec((tr, d), lambda i: (i, 0)), vec, vec],
        out_specs=list(_split_specs(tr, d, n_a // tr)),
        out_shape=[jax.ShapeDtypeStruct((n_a, d), F32), jax.ShapeDtypeStruct((t - n_a, d), F32)],
        compiler_params=_params(("arbitrary",)),
        name="layer_norm_out",
    )(y, g.reshape(1, d), b.reshape(1, d))


def _stream_in_body(xa_ref, xb_ref, of_ref, ob_ref, *, na_tiles):
    i = pl.program_id(0)

    @pl.when(i < na_tiles)
    def _():
        of_ref[...] = xa_ref[...]
        ob_ref[...] = xa_ref[...].astype(BF16)

    @pl.when(i >= na_tiles)
    def _():
        of_ref[...] = xb_ref[...]
        ob_ref[...] = xb_ref[...].astype(BF16)


def _stream_in(xa, xb, cfg):
    (n_a, d), n_b = xa.shape, xb.shape[0]
    t = n_a + n_b
    tr = min(cfg.t_row, n_a, n_b)
    assert n_a % tr == 0 and n_b % tr == 0
    row = pl.BlockSpec((tr, d), lambda i: (i, 0))
    return pl.pallas_call(
        functools.partial(_stream_in_body, na_tiles=n_a // tr),
        grid=(t // tr,),
        in_specs=list(_split_specs(tr, d, n_a // tr)),
        out_specs=[row, row],
        out_shape=[jax.ShapeDtypeStruct((t, d), F32), jax.ShapeDtypeStruct((t, d), BF16)],
        compiler_params=_params(("arbitrary",)),
        name="stream_in",
    )(xa, xb)


def _mlstm_body(q_ref, k_ref, v_ref, g_ref, *rest, chunk, nc, heads, dqk, dv, reverse, reset_chunks, m_init,
                k_scale, combine_eps):
    if combine_eps is None:
        h_ref, c_scr, n_scr, m_scr = rest
    else:
        hf_ref, og_ref, ng_ref, h_ref, c_scr, n_scr, m_scr = rest
    c = pl.program_id(1)
    cc = nc - 1 - c if reverse else c

    @pl.when(functools.reduce(jnp.logical_or, [cc == r for r in reset_chunks]))
    def _():
        c_scr[...] = jnp.zeros_like(c_scr)
        n_scr[...] = jnp.zeros_like(n_scr)
        m_scr[...] = jnp.full_like(m_scr, m_init)

    row = lax.broadcasted_iota(jnp.int32, (chunk, chunk), 0)
    col = lax.broadcasted_iota(jnp.int32, (chunk, chunk), 1)
    visible = (col >= row) if reverse else (col <= row)
    visible_t = (row >= col) if reverse else (row <= col)
    ri, rf = (2, 3) if reverse else (0, 1)

    for hh in range(heads):
        qk_cols = slice(hh * dqk, (hh + 1) * dqk)
        v_cols = slice(hh * dv, (hh + 1) * dv)
        gates = g_ref[hh]
        i_row = gates[ri:ri + 1, :]
        lf_row = jax.nn.log_sigmoid(gates[rf:rf + 1, :])
        b_col = jnp.sum(jnp.where(visible, lf_row, 0.0), axis=-1, keepdims=True)
        lf_col = jnp.sum(jnp.where(row == col, lf_row, 0.0), axis=-1, keepdims=True)
        b_row = jnp.sum(jnp.where(visible_t, lf_col, 0.0), axis=0, keepdims=True)

        m_prev = m_scr[hh, 0:1, 0:1]
        logd = jnp.where(visible, b_col - b_row + i_row, -jnp.inf)
        inter = b_col + m_prev
        m_row = jnp.maximum(jnp.max(logd, axis=-1, keepdims=True), inter)
        dmat = jnp.exp(logd - m_row)

        q = q_ref[:, qk_cols]
        k_f32 = k_ref[:, qk_cols].astype(F32) * k_scale
        k = k_f32.astype(BF16)
        v = v_ref[:, v_cols]
        s = lax.dot_general(q, k, (((1,), (1,)), ((), ())), preferred_element_type=F32) * dmat
        w_inter = jnp.exp(inter - m_row)
        ct = c_scr[hh]
        qc = jnp.dot(q, ct.astype(BF16), preferred_element_type=F32)
        num = jnp.dot(s.astype(BF16), v, preferred_element_type=F32) + w_inter * qc
        qn = jnp.dot(q, n_scr[hh].astype(BF16), preferred_element_type=F32)[:, 0:1]
        den = jnp.sum(s, axis=-1, keepdims=True) + w_inter * qn
        h = num / jnp.maximum(jnp.abs(den), jnp.exp(-m_row))
        if combine_eps is None:
            h_ref[:, v_cols] = h
        else:
            h = h + hf_ref[:, v_cols]
            h = h * lax.rsqrt(jnp.mean(h * h, axis=-1, keepdims=True) + combine_eps)
            h_ref[:, v_cols] = (h * ng_ref[:, v_cols] * og_ref[:, v_cols].astype(F32)).astype(h_ref.dtype)

        bl = jnp.sum(lf_row, axis=-1, keepdims=True)
        a_row = bl - b_row + i_row
        m_new = jnp.maximum(bl + m_prev, jnp.max(a_row, axis=-1, keepdims=True))
        decay = jnp.exp(bl + m_prev - m_new)
        wk_t = k_f32.T * jnp.exp(a_row - m_new)
        c_scr[hh] = decay * ct + jnp.dot(wk_t.astype(BF16), v, preferred_element_type=F32)
        n_scr[hh] = decay * n_scr[hh] + jnp.sum(wk_t, axis=-1, keepdims=True)
        m_scr[hh] = jnp.broadcast_to(m_new, m_scr.shape[1:])


def _mlstm_scan(qkv, gates, seqs, cfg, *, reverse, h_other=None, out_gate=None, norm_g=None):
    t = qkv.shape[0]
    nh, dqk, dv, chunk = cfg.ml_heads, cfg.ml_dqk, cfg.ml_dv, cfg.ml_chunk
    combine = h_other is not None
    hp = min(cfg.ml_heads_per_step_combine if combine else cfg.ml_heads_per_step, nh)
    chunk = min(chunk, min(n for _, n in seqs))
    assert all(s % chunk == 0 and n % chunk == 0 for s, n in seqs)
    nc = t // chunk
    if reverse:
        resets = tuple((s + n) // chunk - 1 for s, n in seqs)
    else:
        resets = tuple(s // chunk for s, _ in seqs)

    def ci(c):
        return nc - 1 - c if reverse else c

    assert nh % hp == 0 and (2 * nh * dqk) % (hp * dv) == 0
    ng = nh // hp
    v_off = 2 * nh * dqk // (hp * dv)
    h_spec = pl.BlockSpec((chunk, hp * dv), lambda h, c: (ci(c), h))
    in_specs = [pl.BlockSpec((chunk, hp * dqk), lambda h, c: (ci(c), h)),
                pl.BlockSpec((chunk, hp * dqk), lambda h, c: (ci(c), ng + h)),
                pl.BlockSpec((chunk, hp * dv), lambda h, c: (ci(c), v_off + h)),
                pl.BlockSpec((hp, 8, chunk), lambda h, c: (h, 0, ci(c)))]
    operands = [qkv, qkv, qkv, gates]
    if combine:
        in_specs += [h_spec, h_spec, pl.BlockSpec((1, hp * dv), lambda h, c: (0, h))]
        operands += [h_other, out_gate, norm_g.reshape(1, nh * dv)]
    return pl.pallas_call(
        functools.partial(_mlstm_body, chunk=chunk, nc=nc, heads=hp, dqk=dqk, dv=dv, reverse=reverse,
                          reset_chunks=resets, m_init=cfg.ml_m_init, k_scale=dqk ** -0.5,
                          combine_eps=cfg.rms_eps if combine else None),
        grid=(ng, nc),
        in_specs=in_specs,
        out_specs=h_spec,
        out_shape=jax.ShapeDtypeStruct((t, nh * dv), BF16 if combine else F32),
        scratch_shapes=[pltpu.VMEM((hp, dqk, dv), F32), pltpu.VMEM((hp, dqk, LANE), F32),
                        pltpu.VMEM((hp, 8, LANE), F32)],
        compiler_params=_params(("parallel", "arbitrary")),
        name="mlstm_scan_bw" if reverse else "mlstm_scan_fw",
    )(*operands)


def _mlstm_mixer(xf, xb, w_in, b_gates, norm_g, w_out, seqs, cfg):
    t = xf.shape[0]
    nh, dqk, dv = cfg.ml_heads, cfg.ml_dqk, cfg.ml_dv
    qk_w, v_w = nh * dqk, nh * dv
    qkv_w = 2 * qk_w + v_w
    main_w = qkv_w + v_w
    n_gates = 4 * nh
    w_qkv = w_in[:, :qkv_w].astype(BF16)
    w_o = w_in[:, qkv_w:main_w].astype(BF16)
    w_gate = jnp.pad(w_in[:, main_w:], ((0, 0), (0, LANE - n_gates))).astype(BF16)
    b_gate = jnp.pad(b_gates, (0, LANE - n_gates)).reshape(1, LANE)

    tm, tn, tk = cfg.mm_wide
    tm, tn = min(tm, t), min(tn, v_w)
    qkv = _mm_plain(xb, w_qkv, cfg.mm_wide, BF16, "mlstm_in_proj")
    out_gate = _matmul(xb, w_o, tm=tm, tn=tn, tk=tk, epilogue=_ep_sigmoid,
                       out_shape=jax.ShapeDtypeStruct((t, v_w), BF16),
                       out_specs=pl.BlockSpec((tm, tn), lambda i, j, k: (i, j)), name="mlstm_out_gate_proj")
    gates = _matmul(xb, w_gate, tm=tm, tn=LANE, tk=tk, epilogue=_ep_bias,
                    out_shape=jax.ShapeDtypeStruct((t, LANE), F32),
                    out_specs=pl.BlockSpec((tm, LANE), lambda i, j, k: (i, 0)),
                    extra=(b_gate,), extra_specs=(pl.BlockSpec((1, LANE), lambda i, j, k: (0, 0)),),
                    name="mlstm_gate_proj")
    g = gates[:, :n_gates].reshape(t, 4, nh).transpose(2, 1, 0)
    g = jnp.pad(g, ((0, 0), (0, 4), (0, 0)))
    h_fw = _mlstm_scan(qkv, g, seqs, cfg, reverse=False)
    hact = _mlstm_scan(qkv, g, seqs, cfg, reverse=True, h_other=h_fw, out_gate=out_gate, norm_g=norm_g)
    return _mm_residual(hact, w_out.astype(BF16), xf, cfg.mm_wide, cfg, "mlstm_out_proj")


def _rope_tables(max_len, cfg, scale):
    half = cfg.rope // 2
    inv_freq = jnp.power(cfg.rope_base, -jnp.arange(0, cfg.rope, 2, dtype=F32) / cfg.rope)
    ang = jnp.arange(max_len, dtype=F32)[:, None] * inv_freq[None, :]
    cos, sin = jnp.cos(ang), jnp.sin(ang)
    z = jnp.zeros((max_len, LANE - 2 * half), F32)
    zh = jnp.zeros((max_len, half), F32)
    cos_t = jnp.concatenate([cos, cos, z], axis=1) * scale
    sin_a = jnp.concatenate([-sin, zh, z], axis=1) * scale
    sin_b = jnp.concatenate([zh, sin, z], axis=1) * scale
    return cos_t, sin_a, sin_b


def _rotate(x, cos_t, sin_a, sin_b, half):
    return x * cos_t + pltpu.roll(x, LANE - half, 1) * sin_a + pltpu.roll(x, half, 1) * sin_b


def _ep_mla_in(acc, extra, outs, ij, *, q_rank, kv_rank, eps):
    gq_ref, gkv_ref = extra
    cq_ref, ckv_ref, kr_ref = outs

    def rms(x, g):
        return x * lax.rsqrt(jnp.mean(x * x, axis=-1, keepdims=True) + eps) * g

    cq_ref[...] = rms(acc[:, :q_rank], gq_ref[...]).astype(cq_ref.dtype)
    ckv_ref[...] = rms(acc[:, q_rank:q_rank + kv_rank], gkv_ref[...]).astype(ckv_ref.dtype)
    kr_ref[...] = acc[:, q_rank + kv_rank:]


def _ep_q_up(acc, extra, outs, ij, *, heads, scale, half):
    cos_ref, sa_ref, sb_ref = extra
    out = outs[0]
    for hh in range(heads):
        lo = hh * 2 * LANE
        out[:, lo:lo + LANE] = (acc[:, lo:lo + LANE] * scale).astype(out.dtype)
        r = acc[:, lo + LANE:lo + 2 * LANE]
        out[:, lo + LANE:lo + 2 * LANE] = _rotate(r, cos_ref[...], sa_ref[...], sb_ref[...], half).astype(out.dtype)


def _kv_up_body(x_ref, wk_ref, wv_ref, kr_ref, cos_ref, sa_ref, sb_ref, k_out, v_out, *, heads, half):
    x = x_ref[...]
    kk = jnp.dot(x, wk_ref[...], preferred_element_type=F32)
    vv = jnp.dot(x, wv_ref[...], preferred_element_type=F32)
    kr = _rotate(kr_ref[...], cos_ref[...], sa_ref[...], sb_ref[...], half).astype(k_out.dtype)
    ones = jnp.ones((x.shape[0], LANE), v_out.dtype)
    for hh in range(heads):
        lo = hh * 2 * LANE
        k_out[:, lo:lo + LANE] = kk[:, hh * LANE:(hh + 1) * LANE].astype(k_out.dtype)
        k_out[:, lo + LANE:lo + 2 * LANE] = kr
        v_out[:, lo:lo + LANE] = vv[:, hh * LANE:(hh + 1) * LANE].astype(v_out.dtype)
        v_out[:, lo + LANE:lo + 2 * LANE] = ones


def _pos_block(seqs, tm):
    def f(i):
        out = i
        for s, _ in seqs:
            out = jnp.where(i >= s // tm, i - s // tm, out)
        return out
    return f


def _attn_body(q_ref, k_ref, v_ref, o_ref, m_scr, acc_scr, *, nkv, heads, qw, vd):
    ki = pl.program_id(3)

    @pl.when(ki == 0)
    def _():
        m_scr[...] = jnp.full_like(m_scr, -jnp.inf)
        acc_scr[...] = jnp.zeros_like(acc_scr)

    tkv = k_ref.shape[0]
    for hh in range(heads):
        q = q_ref[:, hh * qw:(hh + 1) * qw]
        k = k_ref[:, hh * qw:(hh + 1) * qw]
        v = v_ref[:, hh * 2 * vd:(hh + 1) * 2 * vd]
        s = lax.dot_general(q, k, (((1,), (1,)), ((), ())), preferred_element_type=F32)
        m_prev = m_scr[hh]
        m_new = jnp.maximum(m_prev, jnp.broadcast_to(jnp.max(s, axis=-1, keepdims=True), m_prev.shape))
        alpha = jnp.exp2(m_prev - m_new)
        p = jnp.exp2(s - jnp.concatenate([m_new] * (tkv // LANE), axis=1))
        pv = jnp.dot(p.astype(v.dtype), v, preferred_element_type=F32)
        acc_scr[hh] = jnp.concatenate([alpha] * (2 * vd // LANE), axis=1) * acc_scr[hh] + pv
        m_scr[hh] = m_new

    @pl.when(ki == nkv - 1)
    def _():
        for hh in range(heads):
            o_ref[:, hh * vd:(hh + 1) * vd] = (acc_scr[hh, :, :vd] / acc_scr[hh, :, vd:]).astype(o_ref.dtype)


def _attention(q, k, v, row0, nseq, slen, cfg):
    nh, vd, qw = cfg.mla_heads, cfg.vdim, cfg.qk_pad
    assert vd == LANE
    hp = min(cfg.attn_heads_per_step, nh)
    tq, tkv = min(cfg.tq, slen), min(cfg.tkv, slen)
    assert slen % tq == 0 and slen % tkv == 0 and row0 % tq == 0 and row0 % tkv == 0 and nh % hp == 0
    nq, nkv = slen // tq, slen // tkv
    q0, k0 = row0 // tq, row0 // tkv
    return pl.pallas_call(
        functools.partial(_attn_body, nkv=nkv, heads=hp, qw=qw, vd=vd),
        grid=(nseq, nh // hp, nq, nkv),
        in_specs=[pl.BlockSpec((tq, hp * qw), lambda b, h, qi, ki: (q0 + b * nq + qi, h)),
                  pl.BlockSpec((tkv, hp * qw), lambda b, h, qi, ki: (k0 + b * nkv + ki, h)),
                  pl.BlockSpec((tkv, hp * 2 * vd), lambda b, h, qi, ki: (k0 + b * nkv + ki, h))],
        out_specs=pl.BlockSpec((tq, hp * vd), lambda b, h, qi, ki: (b * nq + qi, h)),
        out_shape=jax.ShapeDtypeStruct((nseq * slen, nh * vd), BF16),
        scratch_shapes=[pltpu.VMEM((hp, tq, LANE), F32), pltpu.VMEM((hp, tq, 2 * vd), F32)],
        compiler_params=_params(("parallel", "parallel", "parallel", "arbitrary")),
        name="mla_attention",
    )(q, k, v)


def _mla_mixer(xf, xb, w_in, q_norm_g, kv_norm_g, w_uq, w_ukv, w_out, seqs, cfg):
    t, d = xf.shape
    nh, nope, rope, vd = cfg.mla_heads, cfg.nope, cfg.rope, cfg.vdim
    qr, kvr = cfg.q_rank, cfg.kv_rank
    half = rope // 2
    assert nope == LANE and vd == LANE and rope <= LANE
    qw = cfg.qk_pad
    tm = min(cfg.mm_q_up[0], t)
    tm_in = min(cfg.mm_mla_in[0], t)
    max_len = max(n for _, n in seqs)
    pos = _pos_block(seqs, tm)

    w_in_p = jnp.pad(w_in, ((0, 0), (0, LANE - rope))).astype(BF16)
    w_uq_p = jnp.pad(w_uq.reshape(qr, nh, nope + rope), ((0, 0), (0, 0), (0, qw - nope - rope)))
    w_uq_p = w_uq_p.reshape(qr, nh * qw).astype(BF16)
    w_ukv_h = w_ukv.reshape(kvr, nh, nope + vd)
    w_uk = w_ukv_h[:, :, :nope].reshape(kvr, nh * nope).astype(BF16)
    w_uv = w_ukv_h[:, :, nope:].reshape(kvr, nh * vd).astype(BF16)

    n_in = qr + kvr + LANE
    cq, ckv, kr = _matmul(
        xb, w_in_p, tm=tm_in, tn=n_in, tk=cfg.mm_mla_in[2],
        epilogue=functools.partial(_ep_mla_in, q_rank=qr, kv_rank=kvr, eps=cfg.rms_eps),
        out_shape=[jax.ShapeDtypeStruct((t, qr), BF16), jax.ShapeDtypeStruct((t, kvr), BF16),
                   jax.ShapeDtypeStruct((t, LANE), F32)],
        out_specs=[pl.BlockSpec((tm_in, qr), lambda i, j, k: (i, 0)),
                   pl.BlockSpec((tm_in, kvr), lambda i, j, k: (i, 0)),
                   pl.BlockSpec((tm_in, LANE), lambda i, j, k: (i, 0))],
        extra=(q_norm_g.reshape(1, qr), kv_norm_g.reshape(1, kvr)),
        extra_specs=(pl.BlockSpec((1, qr), lambda i, j, k: (0, 0)),
                     pl.BlockSpec((1, kvr), lambda i, j, k: (0, 0))),
        name="mla_in_proj")

    scale = (nope + rope) ** -0.5 * LOG2_E
    q_tabs = _rope_tables(max_len, cfg, scale)
    k_tabs = _rope_tables(max_len, cfg, 1.0)
    tn = min(cfg.mm_q_up[1], nh * qw)
    tab3 = pl.BlockSpec((tm, LANE), lambda i, j, k: (pos(i), 0))
    q = _matmul(cq, w_uq_p, tm=tm, tn=tn, tk=cfg.mm_q_up[2],
                epilogue=functools.partial(_ep_q_up, heads=tn // qw, scale=scale, half=half),
                out_shape=jax.ShapeDtypeStruct((t, nh * qw), BF16),
                out_specs=pl.BlockSpec((tm, tn), lambda i, j, k: (i, j)),
                extra=q_tabs, extra_specs=(tab3, tab3, tab3), name="mla_q_up")

    hb = min(cfg.kv_heads_per_step, nh)
    tab2 = pl.BlockSpec((tm, LANE), lambda i, j: (pos(i), 0))
    k, v = pl.pallas_call(
        functools.partial(_kv_up_body, heads=hb, half=half),
        grid=(t // tm, nh // hb),
        in_specs=[pl.BlockSpec((tm, kvr), lambda i, j: (i, 0)),
                  pl.BlockSpec((kvr, hb * nope), lambda i, j: (0, j)),
                  pl.BlockSpec((kvr, hb * vd), lambda i, j: (0, j)),
                  pl.BlockSpec((tm, LANE), lambda i, j: (i, 0)),
                  tab2, tab2, tab2],
        out_specs=[pl.BlockSpec((tm, hb * qw), lambda i, j: (i, j)),
                   pl.BlockSpec((tm, hb * 2 * vd), lambda i, j: (i, j))],
        out_shape=[jax.ShapeDtypeStruct((t, nh * qw), BF16), jax.ShapeDtypeStruct((t, nh * 2 * vd), BF16)],
        compiler_params=_params(("parallel", "parallel")),
        name="mla_kv_up",
    )(ckv, w_uk, w_uv, kr, *k_tabs)

    outs = []
    idx = 0
    while idx < len(seqs):
        row0, slen = seqs[idx]
        nseq = 1
        while idx + nseq < len(seqs) and seqs[idx + nseq] == (row0 + nseq * slen, slen):
            nseq += 1
        outs.append(_attention(q, k, v, row0, nseq, slen, cfg))
        idx += nseq
    o = outs[0] if len(outs) == 1 else jnp.concatenate(outs, axis=0)
    return _mm_residual(o, w_out.astype(BF16), xf, cfg.mm_wide, cfg, "mla_out_proj")


def _ffn_up_body(x_ref, wg_ref, wu_ref, hg_ref, hu_ref, cwg_ref, cwu_ref, cbg_ref, cbu_ref, out_ref, *,
                 tm, seq_starts, seq_ends):
    r0 = pl.program_id(0) * tm
    at_start = functools.reduce(jnp.logical_or, [r0 == s for s in seq_starts])
    at_end = functools.reduce(jnp.logical_or, [r0 + tm == e for e in seq_ends])
    x = x_ref[...]

    def conv(w_ref, halo_ref, cw_ref, cb_ref):
        h = jnp.dot(x, w_ref[...], preferred_element_type=F32)
        prev_row = jnp.where(at_start, 0.0, halo_ref[0:1, :])
        next_row = jnp.where(at_end, 0.0, halo_ref[1:2, :])
        h_prev = jnp.concatenate([prev_row, h[:tm - 1, :]], axis=0)
        h_next = jnp.concatenate([h[1:, :], next_row], axis=0)
        cw = cw_ref[...]
        return h_prev * cw[0:1, :] + h * cw[1:2, :] + h_next * cw[2:3, :] + cb_ref[...]

    g = conv(wg_ref, hg_ref, cwg_ref, cbg_ref)
    u = conv(wu_ref, hu_ref, cwu_ref, cbu_ref)
    out_ref[...] = (g * jax.nn.sigmoid(g) * u).astype(out_ref.dtype)


def _up_weight_body(w_ref, o_ref, *, f, fp):
    rows = w_ref.shape[0]
    o_ref[:, 0:f] = w_ref[:, 0:f].astype(BF16)
    o_ref[:, f:fp] = jnp.zeros((rows, fp - f), BF16)
    o_ref[:, fp:fp + f] = w_ref[:, f:2 * f].astype(BF16)
    o_ref[:, fp + f:2 * fp] = jnp.zeros((rows, fp - f), BF16)


def _down_weight_body(w_ref, o_ref, *, n_src_tiles):
    i = pl.program_id(0)

    @pl.when(i < n_src_tiles)
    def _():
        o_ref[...] = w_ref[...].astype(BF16)

    @pl.when(i >= n_src_tiles)
    def _():
        o_ref[...] = jnp.zeros_like(o_ref)


def _ffn_weights(w_up, w_down, layer, cfg):
    _, d, _ = w_up.shape
    f, fp = cfg.d_ff, cfg.d_ff_pad
    assert f % LANE == 0 and fp % LANE == 0
    tr = min(cfg.t_wprep, d)
    assert d % tr == 0
    up = pl.pallas_call(
        functools.partial(_up_weight_body, f=f, fp=fp),
        grid=(d // tr,),
        in_specs=[pl.BlockSpec((None, tr, 2 * f), lambda i: (layer, i, 0))],
        out_specs=pl.BlockSpec((tr, 2 * fp), lambda i: (i, 0)),
        out_shape=jax.ShapeDtypeStruct((d, 2 * fp), BF16),
        compiler_params=_params(("parallel",)),
        name="ffn_up_weight_prep",
    )(w_up)

    tk = fp - f
    assert f % tk == 0 and tk % 16 == 0
    n_src = f // tk
    down = pl.pallas_call(
        functools.partial(_down_weight_body, n_src_tiles=n_src),
        grid=(fp // tk,),
        in_specs=[pl.BlockSpec((None, tk, d), lambda i: (layer, jnp.minimum(i, n_src - 1), 0))],
        out_specs=pl.BlockSpec((tk, d), lambda i: (i, 0)),
        out_shape=jax.ShapeDtypeStruct((fp, d), BF16),
        compiler_params=_params(("arbitrary",)),
        name="ffn_down_weight_prep",
    )(w_down)
    return up, down


def _conv_ffn(xf, xb, w_up_p, conv_w, conv_b, w_down_p, seqs, cfg):
    t, d = xb.shape
    f, fp, cbw = cfg.d_ff, cfg.d_ff_pad, cfg.ff_block
    pad = fp - f
    nb = fp // cbw
    assert fp % cbw == 0

    def pad_halves(a):
        lead = [(0, 0)] * (a.ndim - 1)
        return jnp.concatenate([jnp.pad(a[..., :f], lead + [(0, pad)]), jnp.pad(a[..., f:], lead + [(0, pad)])], -1)

    cw, cb = pad_halves(conv_w), pad_halves(conv_b.reshape(1, -1))

    tm, _, tk = cfg.mm_wide
    tm = min(tm, t)
    nt = t // tm
    assert tk == d
    halo_idx = np.array([[max(i * tm - 1, 0), min((i + 1) * tm, t - 1)] * 4 for i in range(nt)], np.int32).reshape(-1)
    h_halo = _mm_plain(xb[halo_idx], w_up_p, (nt * 8, cfg.mm_halo_tn, tk), F32, "ffn_up_halo")

    def gate_cols(rows):
        return pl.BlockSpec((rows, cbw), lambda i, j: (0, j))

    def up_cols(rows):
        return pl.BlockSpec((rows, cbw), lambda i, j: (0, nb + j))

    act = pl.pallas_call(
        functools.partial(_ffn_up_body, tm=tm,
                          seq_starts=tuple(s for s, _ in seqs), seq_ends=tuple(s + n for s, n in seqs)),
        grid=(nt, nb),
        in_specs=[pl.BlockSpec((tm, d), lambda i, j: (i, 0)),
                  gate_cols(d), up_cols(d),
                  pl.BlockSpec((8, cbw), lambda i, j: (i, j)), pl.BlockSpec((8, cbw), lambda i, j: (i, nb + j)),
                  gate_cols(3), up_cols(3), gate_cols(1), up_cols(1)],
        out_specs=pl.BlockSpec((tm, cbw), lambda i, j: (i, j)),
        out_shape=jax.ShapeDtypeStruct((t, fp), BF16),
        compiler_params=_params(("parallel", "parallel")),
        name="ffn_up_conv_gate",
    )(xb, w_up_p, w_up_p, h_halo, h_halo, cw, cw, cb, cb)
    return _mm_residual(act, w_down_p, xf, cfg.mm_down, cfg, "ffn_down")


def _trunk(xf, xb, n_first, seqs, cfg, ml_w_in, ml_b_gates, ml_norm_g, ml_w_out, mla_w_in, mla_q_norm_g,
           mla_kv_norm_g, mla_w_uq, mla_w_ukv, mla_w_out, ffn_w_up, ffn_conv_w, ffn_conv_b, ffn_w_down,
           ln1_g, ln1_b, ln2_g, ln2_b):
    for i in range(cfg.depth):
        j = i // 2
        if i % 2 == 0:
            y = _mlstm_mixer(xf, xb, ml_w_in[j], ml_b_gates[j], ml_norm_g[j], ml_w_out[j], seqs, cfg)
        else:
            y = _mla_mixer(xf, xb, mla_w_in[j], mla_q_norm_g[j], mla_kv_norm_g[j],
                           mla_w_uq[j], mla_w_ukv[j], mla_w_out[j], seqs, cfg)
        xf, xb = _layer_norm(y, ln1_g[i], ln1_b[i], cfg)
        w_up_p, w_down_p = _ffn_weights(ffn_w_up, ffn_w_down, i, cfg)
        y = _conv_ffn(xf, xb, w_up_p, ffn_conv_w[i], ffn_conv_b[i], w_down_p, seqs, cfg)
        if i == cfg.depth - 1:
            return _layer_norm_split(y, ln2_g[i], ln2_b[i], n_first, cfg)
        xf, xb = _layer_norm(y, ln2_g[i], ln2_b[i], cfg)


def _run(cfg, x_prompt, x_sample, *weights):
    d = x_prompt.shape[-1]
    shapes = [x_prompt.shape, x_sample.shape]
    seqs, row = [], 0
    for b, s, _ in shapes:
        for _ in range(b):
            seqs.append((row, s))
            row += s
    xf, xb = _stream_in(x_prompt.reshape(-1, d), x_sample.reshape(-1, d), cfg)
    n_prompt = shapes[0][0] * shapes[0][1]
    y_prompt, y_sample = _trunk(xf, xb, n_prompt, tuple(seqs), cfg, *weights)
    return y_prompt.reshape(shapes[0]), y_sample.reshape(shapes[1])


def kernel(x_prompt, x_sample, ml_w_in, ml_b_gates, ml_norm_g, ml_w_out, mla_w_in, mla_q_norm_g, mla_kv_norm_g, mla_w_uq, mla_w_ukv, mla_w_out, ffn_w_up, ffn_conv_w, ffn_conv_b, ffn_w_down, ln1_g, ln1_b, ln2_g, ln2_b):
    return _run(Cfg(), x_prompt, x_sample, ml_w_in, ml_b_gates, ml_norm_g, ml_w_out, mla_w_in, mla_q_norm_g,
                mla_kv_norm_g, mla_w_uq, mla_w_ukv, mla_w_out, ffn_w_up, ffn_conv_w, ffn_conv_b, ffn_w_down,
                ln1_g, ln1_b, ln2_g, ln2_b)
```
